```python
import math
import jax
import jax.numpy as jnp
from jax import lax
import numpy as np


D_MODEL = 1024
BATCH = 4
SEQ = 8192
DEPTH = 1

POOL_WIDTH = D_MODEL // 2
POOL_WINDOWS = (2, 4, 8, 16)
N_POOL_GROUPS = len(POOL_WINDOWS)
POOL_GROUP_DIM = POOL_WIDTH // N_POOL_GROUPS

DIFF_HEAD_DIM = 64
N_DIFF_HEADS = D_MODEL // (4 * DIFF_HEAD_DIM)
DIFF_QK_WIDTH = N_DIFF_HEADS * 2 * DIFF_HEAD_DIM
DIFF_V_WIDTH = N_DIFF_HEADS * 2 * DIFF_HEAD_DIM

IN_WIDTH = POOL_WIDTH + 2 * DIFF_QK_WIDTH + DIFF_V_WIDTH + 2 * D_MODEL
FFN_HIDDEN = int(math.ceil((8 * D_MODEL / 3) / 256) * 256)
Q_BLOCK = 128
NORM_EPS = 1e-6

kernel_name = 'hybrid_pool_diffattn_gated_block'


def rms_norm(x, g):
    xf = x.astype(jnp.float32)
    y = xf * lax.rsqrt(jnp.mean(xf * xf, axis=-1, keepdims=True) + NORM_EPS)
    return (y * g.astype(jnp.float32)).astype(x.dtype)


def alibi_slopes(n):
    start = 2.0 ** (-8.0 / n)
    return jnp.asarray(np.array([start ** (i + 1) for i in range(n)], dtype=np.float32))


def pool_mixer(p, pool_mix, pool_scale):
    B, S, _ = p.shape
    pg = p.reshape(B, S, N_POOL_GROUPS, POOL_GROUP_DIM).astype(jnp.float32)
    cs = jnp.cumsum(pg, axis=1)
    t = jnp.arange(S)
    outs = []
    for g, w in enumerate(POOL_WINDOWS):
        c = cs[:, :, g]
        prev = jnp.pad(c, ((0, 0), (w, 0), (0, 0)))[:, :S]
        cnt = jnp.minimum(t + 1, w).astype(jnp.float32)[None, :, None]
        outs.append((c - prev) / cnt - pg[:, :, g])
    y = jnp.stack(outs, axis=2).astype(p.dtype)
    y = jnp.einsum('bsgc,gcd->bsgd', y, pool_mix)
    return y.reshape(B, S, POOL_WIDTH) * pool_scale


def diff_attention(q, k, v, lam_q1, lam_k1, lam_q2, lam_k2, subln_g, lambda_init):
    B, S, _ = q.shape
    H, d = N_DIFF_HEADS, DIFF_HEAD_DIM
    q = q.reshape(B, S, H, 2, d).transpose(0, 2, 3, 1, 4)
    k = k.reshape(B, S, H, 2, d).transpose(0, 2, 3, 1, 4)
    vh = v.reshape(B, S, H, 2 * d).transpose(0, 2, 1, 3)
    k1, k2 = k[:, :, 0], k[:, :, 1]
    lam = (jnp.exp(jnp.sum(lam_q1.astype(jnp.float32) * lam_k1.astype(jnp.float32)))
           - jnp.exp(jnp.sum(lam_q2.astype(jnp.float32) * lam_k2.astype(jnp.float32)))
           + lambda_init)
    slopes = alibi_slopes(H)
    scale = 1.0 / math.sqrt(d)
    nblk = S // Q_BLOCK
    q1b = q[:, :, 0].reshape(B, H, nblk, Q_BLOCK, d).transpose(2, 0, 1, 3, 4)
    q2b = q[:, :, 1].reshape(B, H, nblk, Q_BLOCK, d).transpose(2, 0, 1, 3, 4)
    starts = jnp.arange(nblk) * Q_BLOCK
    kpos = jnp.arange(S)

    def block(args):
        qa, qb, start = args
        dist = (start + jnp.arange(Q_BLOCK))[:, None] - kpos[None, :]
        causal = dist >= 0
        bias = -slopes[:, None, None] * dist.astype(jnp.float32)[None]

        def probs(qq, kk):
            s = jnp.einsum('bhqd,bhkd->bhqk', qq, kk).astype(jnp.float32) * scale + bias
            s = jnp.where(causal, s, -jnp.inf)
            return jax.nn.softmax(s, axis=-1)

        a = probs(qa, k1) - lam * probs(qb, k2)
        return jnp.einsum('bhqk,bhkd->bhqd', a.astype(vh.dtype), vh)

    o = lax.map(block, (q1b, q2b, starts))
    o = o.transpose(1, 0, 3, 2, 4).reshape(B, S, H, 2 * d)
    o = rms_norm(o, subln_g) * (1.0 - lambda_init)
    return o.reshape(B, S, H * 2 * d)


def hybrid_layer(h, layer_idx, w_in, pool_mix, pool_scale, w_branch_a,
                 lam_q1, lam_k1, lam_q2, lam_k2, subln_g, w_branch_b, w_out,
                 mix_pre_g, mix_post_g, ffn_pre_g, ffn_post_g, w_ffn_gate, w_ffn_up, w_ffn_down):
    lambda_init = 0.8 - 0.6 * math.exp(-0.3 * layer_idx)
    u = rms_norm(h, mix_pre_g)
    z = jnp.einsum('bsd,de->bse', u, w_in)
    o1 = POOL_WIDTH
    o2 = o1 + DIFF_QK_WIDTH
    o3 = o2 + DIFF_QK_WIDTH
    o4 = o3 + DIFF_V_WIDTH
    o5 = o4 + D_MODEL
    p, q, k, v = z[..., :o1], z[..., o1:o2], z[..., o2:o3], z[..., o3:o4]
    gate_a, gate_b = z[..., o4:o5], z[..., o5:]
    ya = jnp.einsum('bsc,cd->bsd', pool_mixer(p, pool_mix, pool_scale), w_branch_a)
    yb = jnp.einsum('bsc,cd->bsd',
                    diff_attention(q, k, v, lam_q1, lam_k1, lam_q2, lam_k2, subln_g, lambda_init),
                    w_branch_b)
    m = jax.nn.sigmoid(gate_a) * ya + jax.nn.sigmoid(gate_b) * yb
    h = h + rms_norm(jnp.einsum('bsd,de->bse', m, w_out), mix_post_g)
    u = rms_norm(h, ffn_pre_g)
    f = jax.nn.silu(jnp.einsum('bsd,df->bsf', u, w_ffn_gate)) * jnp.einsum('bsd,df->bsf', u, w_ffn_up)
    h = h + rms_norm(jnp.einsum('bsf,fd->bsd', f, w_ffn_down), ffn_post_g)
    return h


def setup_inputs(seed: int = 0) -> dict:
    key = jax.random.key(seed)
    ks = jax.random.split(key, 20)
    L = DEPTH
    f32 = jnp.float32

    def nrm(k, shape, fan_in):
        return jax.random.normal(k, shape, f32) * (fan_in ** -0.5)

    def gain(k, shape):
        return 1.0 + 0.05 * jax.random.normal(k, shape, f32)

    return {
        'x': jax.random.normal(ks[0], (BATCH, SEQ, D_MODEL), f32),
        'w_in': nrm(ks[1], (L, D_MODEL, IN_WIDTH), D_MODEL),
        'pool_mix': nrm(ks[2], (L, N_POOL_GROUPS, POOL_GROUP_DIM, POOL_GROUP_DIM), POOL_GROUP_DIM),
        'pool_scale': gain(ks[3], (L, POOL_WIDTH)),
        'w_branch_a': nrm(ks[4], (L, POOL_WIDTH, D_MODEL), POOL_WIDTH),
        'lam_q1': 0.1 * jax.random.normal(ks[5], (L, DIFF_HEAD_DIM), f32),
        'lam_k1': 0.1 * jax.random.normal(ks[6], (L, DIFF_HEAD_DIM), f32),
        'lam_q2': 0.1 * jax.random.normal(ks[7], (L, DIFF_HEAD_DIM), f32),
        'lam_k2': 0.1 * jax.random.normal(ks[8], (L, DIFF_HEAD_DIM), f32),
        'subln_g': gain(ks[9], (L, 2 * DIFF_HEAD_DIM)),
        'w_branch_b': nrm(ks[10], (L, DIFF_V_WIDTH, D_MODEL), DIFF_V_WIDTH),
        'w_out': nrm(ks[11], (L, D_MODEL, D_MODEL), D_MODEL),
        'mix_pre_g': gain(ks[12], (L, D_MODEL)),
        'mix_post_g': gain(ks[13], (L, D_MODEL)),
        'ffn_pre_g': gain(ks[14], (L, D_MODEL)),
        'ffn_post_g': gain(ks[15], (L, D_MODEL)),
        'w_ffn_gate': nrm(ks[16], (L, D_MODEL, FFN_HIDDEN), D_MODEL),
        'w_ffn_up': nrm(ks[17], (L, D_MODEL, FFN_HIDDEN), D_MODEL),
        'w_ffn_down': nrm(ks[18], (L, FFN_HIDDEN, D_MODEL), FFN_HIDDEN),
    }


def reference(x, w_in, pool_mix, pool_scale, w_branch_a, lam_q1, lam_k1, lam_q2, lam_k2,
              subln_g, w_branch_b, w_out, mix_pre_g, mix_post_g, ffn_pre_g, ffn_post_g,
              w_ffn_gate, w_ffn_up, w_ffn_down):
    h = x
    for l in range(DEPTH):
        h = hybrid_layer(h, l, w_in[l], pool_mix[l], pool_scale[l], w_branch_a[l],
                         lam_q1[l], lam_k1[l], lam_q2[l], lam_k2[l], subln_g[l],
                         w_branch_b[l], w_out[l], mix_pre_g[l], mix_post_g[l],
                         ffn_pre_g[l], ffn_post_g[l], w_ffn_gate[l], w_ffn_up[l], w_ffn_down[l])
    return h
```

```python
import functools
import math

import jax
import jax.numpy as jnp
from jax import lax
from jax.experimental import pallas as pl
from jax.experimental.pallas import tpu as pltpu

NORM_EPS = 1e-6
POOL_WINDOWS = (2, 4, 8, 16)
POOL_GROUP_DIM = 128
POOL_HALO = 16
HEAD_DIM = 64
HEAD_WIDTH = 2 * HEAD_DIM
MASK_VALUE = -1e30
BF16 = jnp.bfloat16
F32 = jnp.float32

VMEM_LIMIT_BYTES = 52 * 1024 * 1024


def _rms(x, g):
    ms = jnp.mean(x * x, axis=-1, keepdims=True)
    return x * lax.rsqrt(ms + NORM_EPS) * g


def _proj_kernel(x_ref, g_ref, w_ref, pm_ref, ps_ref, wa_ref,
                 q_ref, k_ref, v_ref, ma_ref, sgb_ref,
                 pbuf_ref, ybuf_ref, *, tm, tiles_per_seq, pool_w, qk_w, v_w, d_model):
    i = pl.program_id(0)
    seq_tile = i % tiles_per_seq
    u = _rms(x_ref[...], g_ref[...]).astype(BF16)

    def proj(lo, width):
        return jnp.dot(u, w_ref[:, lo:lo + width], preferred_element_type=F32)

    o1 = pool_w
    o2 = o1 + qk_w
    o3 = o2 + qk_w
    o4 = o3 + v_w
    o5 = o4 + d_model
    q_ref[...] = (proj(o1, qk_w) * (1.0 / math.sqrt(HEAD_DIM))).astype(BF16)
    k_ref[...] = proj(o2, qk_w).astype(BF16)
    v_ref[...] = proj(o3, v_w).astype(BF16)
    sgb_ref[...] = jax.nn.sigmoid(proj(o5, d_model)).astype(BF16)

    @pl.when(seq_tile == 0)
    def _():
        pbuf_ref[0:POOL_HALO, :] = jnp.zeros((POOL_HALO, pool_w), F32)

    @pl.when(seq_tile != 0)
    def _():
        pbuf_ref[0:POOL_HALO, :] = pbuf_ref[tm:tm + POOL_HALO, :]

    pbuf_ref[POOL_HALO:POOL_HALO + tm, :] = proj(0, pool_w)

    t_loc = seq_tile * tm + lax.broadcasted_iota(jnp.int32, (tm, 1), 0)
    for g, w in enumerate(POOL_WINDOWS):
        cols = slice(g * POOL_GROUP_DIM, (g + 1) * POOL_GROUP_DIM)
        tok = pbuf_ref[POOL_HALO:POOL_HALO + tm, cols]
        acc = tok
        for d in range(1, w):
            acc = acc + pbuf_ref[POOL_HALO - d:POOL_HALO - d + tm, cols]
        cnt = jnp.minimum(t_loc + 1, w).astype(F32)
        pooled = acc / cnt - tok
        y = jnp.dot(pooled.astype(BF16), pm_ref[g], preferred_element_type=F32)
        ybuf_ref[:, cols] = (y * ps_ref[:, cols]).astype(BF16)

    ya = jnp.dot(ybuf_ref[...], wa_ref[...], preferred_element_type=F32)
    ma_ref[...] = (jax.nn.sigmoid(proj(o4, d_model)) * ya).astype(BF16)


def _proj_call(x2, mix_pre_g, w_in, pool_mix, pool_scale, w_branch_a, *, seq, tm):
    t, d_model = x2.shape
    in_w = w_in.shape[1]
    pool_w = pool_scale.shape[1]
    v_w = pool_w
    qk_w = (in_w - pool_w - v_w - 2 * d_model) // 2
    const = lambda i: (0, 0)
    row = lambda i: (i, 0)
    single = pl.Buffered(1)
    kern = functools.partial(_proj_kernel, tm=tm, tiles_per_seq=seq // tm, pool_w=pool_w,
                             qk_w=qk_w, v_w=v_w, d_model=d_model)
    return pl.pallas_call(
        kern,
        grid=(t // tm,),
        in_specs=[
            pl.BlockSpec((tm, d_model), row),
            pl.BlockSpec((1, d_model), const),
            pl.BlockSpec((d_model, in_w), const, pipeline_mode=single),
            pl.BlockSpec(pool_mix.shape, lambda i: (0, 0, 0), pipeline_mode=single),
            pl.BlockSpec((1, pool_w), const),
            pl.BlockSpec((pool_w, d_model), const, pipeline_mode=single),
        ],
        out_specs=[
            pl.BlockSpec((tm, qk_w), row),
            pl.BlockSpec((tm, qk_w), row),
            pl.BlockSpec((tm, v_w), row),
            pl.BlockSpec((tm, d_model), row),
            pl.BlockSpec((tm, d_model), row),
        ],
        out_shape=[
            jax.ShapeDtypeStruct((t, qk_w), BF16),
            jax.ShapeDtypeStruct((t, qk_w), BF16),
            jax.ShapeDtypeStruct((t, v_w), BF16),
            jax.ShapeDtypeStruct((t, d_model), BF16),
            jax.ShapeDtypeStruct((t, d_model), BF16),
        ],
        scratch_shapes=[
            pltpu.VMEM((POOL_HALO + tm, pool_w), F32),
            pltpu.VMEM((tm, pool_w), BF16),
        ],
        compiler_params=pltpu.CompilerParams(
            dimension_semantics=("arbitrary",), vmem_limit_bytes=VMEM_LIMIT_BYTES),
        name="proj_pool",
    )(x2, mix_pre_g, w_in, pool_mix, pool_scale, w_branch_a)


def _attn_kernel(slopes_ref, lq1_ref, lk1_ref, lq2_ref, lk2_ref, sg_ref,
                 q_ref, k_ref, v_ref, o_ref, nbias_ref, *, blk, lambda_init):
    h = pl.program_id(1)
    qi = pl.program_id(2)
    slope = slopes_ref[h]

    lam = (jnp.exp(jnp.sum(lq1_ref[...] * lk1_ref[...], axis=-1, keepdims=True))
           - jnp.exp(jnp.sum(lq2_ref[...] * lk2_ref[...], axis=-1, keepdims=True))
           + lambda_init)

    q = q_ref[...]
    lane = lax.broadcasted_iota(jnp.int32, q.shape, 1)
    zero = jnp.zeros_like(q)
    q1 = jnp.where(lane < HEAD_DIM, q, zero)
    q2 = jnp.where(lane >= HEAD_DIM, q, zero)

    rel = (lax.broadcasted_iota(jnp.int32, (blk, blk), 0)
           - lax.broadcasted_iota(jnp.int32, (blk, blk), 1))
    nbias_ref[...] = -slope * rel.astype(F32)
    causal = rel >= 0

    def softmax_step(s, c, m, l, acc, vb):
        m_new = jnp.maximum(m, jnp.max(s, axis=-1, keepdims=True) + c)
        alpha = jnp.exp(m - m_new)
        p = jnp.exp(s - (m_new - c))
        l_new = alpha * l + jnp.sum(p, axis=-1, keepdims=True)
        acc_new = alpha * acc + jnp.dot(p.astype(BF16), vb, preferred_element_type=F32)
        return m_new, l_new, acc_new

    def block(kj, carry, masked):
        m1, l1, a1, m2, l2, a2 = carry
        start = pl.multiple_of(kj * blk, blk)
        kb = k_ref[pl.ds(start, blk), :]
        vb = v_ref[pl.ds(start, blk), :]
        nt = (((1,), (1,)), ((), ()))
        s1 = lax.dot_general(q1, kb, nt, preferred_element_type=F32) + nbias_ref[...]
        s2 = lax.dot_general(q2, kb, nt, preferred_element_type=F32) + nbias_ref[...]
        if masked:
            s1 = jnp.where(causal, s1, MASK_VALUE)
            s2 = jnp.where(causal, s2, MASK_VALUE)
        c = -slope * ((qi - kj) * blk).astype(F32)
        m1, l1, a1 = softmax_step(s1, c, m1, l1, a1, vb)
        m2, l2, a2 = softmax_step(s2, c, m2, l2, a2, vb)
        return m1, l1, a1, m2, l2, a2

    m0 = jnp.full((blk, 1), MASK_VALUE, F32)
    l0 = jnp.zeros((blk, 1), F32)
    a0 = jnp.zeros((blk, HEAD_WIDTH), F32)
    carry = (m0, l0, a0, m0, l0, a0)
    carry = lax.fori_loop(0, qi, lambda kj, cr: block(kj, cr, False), carry)
    m1, l1, a1, m2, l2, a2 = block(qi, carry, True)

    o = a1 / l1 - lam * (a2 / l2)
    o_ref[...] = (_rms(o, sg_ref[...]) * (1.0 - lambda_init)).astype(BF16)


def _attn_call(q, k, v, slopes, lq1, lk1, lq2, lk2, subln_g, *, blk, lambda_init):
    b, s, width = q.shape
    heads = width // HEAD_WIDTH
    small = lambda bi, hi, qi: (0, 0)
    kern = functools.partial(_attn_kernel, blk=blk, lambda_init=lambda_init)
    return pl.pallas_call(
        kern,
        grid=(b, heads, s // blk),
        in_specs=[
            pl.BlockSpec(memory_space=pltpu.SMEM),
            pl.BlockSpec((1, HEAD_DIM), small),
            pl.BlockSpec((1, HEAD_DIM), small),
            pl.BlockSpec((1, HEAD_DIM), small),
            pl.BlockSpec((1, HEAD_DIM), small),
            pl.BlockSpec((1, HEAD_WIDTH), small),
            pl.BlockSpec((None, blk, HEAD_WIDTH), lambda bi, hi, qi: (bi, qi, hi)),
            pl.BlockSpec((None, s, HEAD_WIDTH), lambda bi, hi, qi: (bi, 0, hi)),
            pl.BlockSpec((None, s, HEAD_WIDTH), lambda bi, hi, qi: (bi, 0, hi)),
        ],
        out_specs=pl.BlockSpec((None, blk, HEAD_WIDTH), lambda bi, hi, qi: (bi, qi, hi)),
        out_shape=jax.ShapeDtypeStruct((b, s, width), BF16),
        scratch_shapes=[pltpu.VMEM((blk, blk), F32)],
        compiler_params=pltpu.CompilerParams(
            dimension_semantics=("parallel", "parallel", "arbitrary"),
            vmem_limit_bytes=VMEM_LIMIT_BYTES),
        name="diff_attn",
    )(slopes, lq1, lk1, lq2, lk2, subln_g, q, k, v)


def _out_ffn_kernel(x_ref, o_ref, ma_ref, sgb_ref, wb_ref, wo_ref, g_post_ref,
                    g_pre_ref, g_fpost_ref, wg_ref, wu_ref, wd_ref, out_ref):
    yb = jnp.dot(o_ref[...], wb_ref[...], preferred_element_type=F32)
    m = ma_ref[...].astype(F32) + sgb_ref[...].astype(F32) * yb
    mo = jnp.dot(m.astype(BF16), wo_ref[...], preferred_element_type=F32)
    h1 = x_ref[...] + _rms(mo, g_post_ref[...])
    u = _rms(h1, g_pre_ref[...]).astype(BF16)
    gate = jnp.dot(u, wg_ref[...], preferred_element_type=F32)
    up = jnp.dot(u, wu_ref[...], preferred_element_type=F32)
    f = (jax.nn.silu(gate) * up).astype(BF16)
    dn = jnp.dot(f, wd_ref[...], preferred_element_type=F32)
    out_ref[...] = h1 + _rms(dn, g_fpost_ref[...])


def _out_ffn_call(x2, o2, ma, sgb, w_branch_b, w_out, mix_post_g, ffn_pre_g, ffn_post_g,
                  w_gate, w_up, w_down, *, tm):
    t, d_model = x2.shape
    const = lambda i: (0, 0)
    row = lambda i: (i, 0)
    single = pl.Buffered(1)

    def wspec(w):
        return pl.BlockSpec(w.shape, const, pipeline_mode=single)

    return pl.pallas_call(
        _out_ffn_kernel,
        grid=(t // tm,),
        in_specs=[
            pl.BlockSpec((tm, d_model), row),
            pl.BlockSpec((tm, o2.shape[1]), row),
            pl.BlockSpec((tm, d_model), row),
            pl.BlockSpec((tm, d_model), row),
            wspec(w_branch_b),
            wspec(w_out),
            pl.BlockSpec((1, d_model), const),
            pl.BlockSpec((1, d_model), const),
            pl.BlockSpec((1, d_model), const),
            wspec(w_gate),
            wspec(w_up),
            wspec(w_down),
        ],
        out_specs=pl.BlockSpec((tm, d_model), row),
        out_shape=jax.ShapeDtypeStruct((t, d_model), F32),
        compiler_params=pltpu.CompilerParams(
            dimension_semantics=("parallel",), vmem_limit_bytes=VMEM_LIMIT_BYTES),
        name="out_ffn",
    )(x2, o2, ma, sgb, w_branch_b, w_out, mix_post_g, ffn_pre_g, ffn_post_g,
      w_gate, w_up, w_down)


def _alibi_slopes(n):
    start = 2.0 ** (-8.0 / n)
    return jnp.asarray([start ** (i + 1) for i in range(n)], dtype=F32)


def _layer(h, layer_idx, w_in, pool_mix, pool_scale, w_branch_a, lam_q1, lam_k1, lam_q2,
           lam_k2, subln_g, w_branch_b, w_out, mix_pre_g, mix_post_g, ffn_pre_g, ffn_post_g,
           w_ffn_gate, w_ffn_up, w_ffn_down):
    b, s, d_model = h.shape
    lambda_init = 0.8 - 0.6 * math.exp(-0.3 * layer_idx)
    x2 = h.reshape(b * s, d_model)
    row = lambda a: a.reshape(1, -1)

    q, k, v, ma, sgb = _proj_call(
        x2, row(mix_pre_g), w_in.astype(BF16), pool_mix.astype(BF16), row(pool_scale),
        w_branch_a.astype(BF16), seq=s, tm=512)

    width = q.shape[1]
    heads = width // HEAD_WIDTH
    o = _attn_call(q.reshape(b, s, width), k.reshape(b, s, width), v.reshape(b, s, width),
                   _alibi_slopes(heads), row(lam_q1), row(lam_k1), row(lam_q2), row(lam_k2),
                   row(subln_g), blk=512, lambda_init=lambda_init)

    out = _out_ffn_call(
        x2, o.reshape(b * s, width), ma, sgb, w_branch_b.astype(BF16), w_out.astype(BF16),
        row(mix_post_g), row(ffn_pre_g), row(ffn_post_g), w_ffn_gate.astype(BF16),
        w_ffn_up.astype(BF16), w_ffn_down.astype(BF16), tm=256)
    return out.reshape(b, s, d_model)


def kernel(x, w_in, pool_mix, pool_scale, w_branch_a, lam_q1, lam_k1, lam_q2, lam_k2,
           subln_g, w_branch_b, w_out, mix_pre_g, mix_post_g, ffn_pre_g, ffn_post_g,
           w_ffn_gate, w_ffn_up, w_ffn_down):
    h = x
    for l in range(w_in.shape[0]):
        h = _layer(h, l, w_in[l], pool_mix[l], pool_scale[l], w_branch_a[l], lam_q1[l],
                   lam_k1[l], lam_q2[l], lam_k2[l], subln_g[l], w_branch_b[l], w_out[l],
                   mix_pre_g[l], mix_post_g[l], ffn_pre_g[l], ffn_post_g[l], w_ffn_gate[l],
                   w_ffn_up[l], w_ffn_down[l])
    return h
```

```python
import functools
import math

import jax
import jax.numpy as jnp
from jax import lax
from jax.experimental import pallas as pl
from jax.experimental.pallas import tpu as pltpu

NORM_EPS = 1e-6
POOL_WINDOWS = (2, 4, 8, 16)
POOL_GROUP_DIM = 128
POOL_HALO = 16
HEAD_DIM = 64
HEAD_WIDTH = 2 * HEAD_DIM
MASK_VALUE = -1e30
AUG_LANES = 128
BF16_EXACT_INT = 256
BF16 = jnp.bfloat16
F32 = jnp.float32

VMEM_LIMIT_BYTES = 52 * 1024 * 1024


def _rms(x, g):
    ms = jnp.mean(x * x, axis=-1, keepdims=True)
    return x * lax.rsqrt(ms + NORM_EPS) * g


def _proj_kernel(x_ref, g_ref, w_ref, pm_ref, ps_ref, wa_ref,
                 q_ref, k_ref, v_ref, ma_ref, sgb_ref,
                 pbuf_ref, ybuf_ref, *, tm, tiles_per_seq, pool_w, qk_w, v_w, d_model):
    i = pl.program_id(0)
    seq_tile = i % tiles_per_seq
    u = _rms(x_ref[...], g_ref[...]).astype(BF16)

    def proj(lo, width):
        return jnp.dot(u, w_ref[:, lo:lo + width], preferred_element_type=F32)

    o1 = pool_w
    o2 = o1 + qk_w
    o3 = o2 + qk_w
    o4 = o3 + v_w
    o5 = o4 + d_model
    q_ref[...] = (proj(o1, qk_w) * (1.0 / math.sqrt(HEAD_DIM))).astype(BF16)
    k_ref[...] = proj(o2, qk_w).astype(BF16)
    v_ref[...] = proj(o3, v_w).astype(BF16)
    sgb_ref[...] = jax.nn.sigmoid(proj(o5, d_model)).astype(BF16)

    @pl.when(seq_tile == 0)
    def _():
        pbuf_ref[0:POOL_HALO, :] = jnp.zeros((POOL_HALO, pool_w), F32)

    @pl.when(seq_tile != 0)
    def _():
        pbuf_ref[0:POOL_HALO, :] = pbuf_ref[tm:tm + POOL_HALO, :]

    pbuf_ref[POOL_HALO:POOL_HALO + tm, :] = proj(0, pool_w)

    t_loc = seq_tile * tm + lax.broadcasted_iota(jnp.int32, (tm, 1), 0)
    for g, w in enumerate(POOL_WINDOWS):
        cols = slice(g * POOL_GROUP_DIM, (g + 1) * POOL_GROUP_DIM)
        tok = pbuf_ref[POOL_HALO:POOL_HALO + tm, cols]
        acc = tok
        for d in range(1, w):
            acc = acc + pbuf_ref[POOL_HALO - d:POOL_HALO - d + tm, cols]
        cnt = jnp.minimum(t_loc + 1, w).astype(F32)
        pooled = acc / cnt - tok
        y = jnp.dot(pooled.astype(BF16), pm_ref[g], preferred_element_type=F32)
        ybuf_ref[:, cols] = (y * ps_ref[:, cols]).astype(BF16)

    ya = jnp.dot(ybuf_ref[...], wa_ref[...], preferred_element_type=F32)
    ma_ref[...] = (jax.nn.sigmoid(proj(o4, d_model)) * ya).astype(BF16)


def _proj_call(x2, mix_pre_g, w_in, pool_mix, pool_scale, w_branch_a, *, seq, tm):
    t, d_model = x2.shape
    in_w = w_in.shape[1]
    pool_w = pool_scale.shape[1]
    v_w = pool_w
    qk_w = (in_w - pool_w - v_w - 2 * d_model) // 2
    const = lambda i: (0, 0)
    row = lambda i: (i, 0)
    single = pl.Buffered(1)
    kern = functools.partial(_proj_kernel, tm=tm, tiles_per_seq=seq // tm, pool_w=pool_w,
                             qk_w=qk_w, v_w=v_w, d_model=d_model)
    return pl.pallas_call(
        kern,
        grid=(t // tm,),
        in_specs=[
            pl.BlockSpec((tm, d_model), row),
            pl.BlockSpec((1, d_model), const),
            pl.BlockSpec((d_model, in_w), const, pipeline_mode=single),
            pl.BlockSpec(pool_mix.shape, lambda i: (0, 0, 0), pipeline_mode=single),
            pl.BlockSpec((1, pool_w), const),
            pl.BlockSpec((pool_w, d_model), const, pipeline_mode=single),
        ],
        out_specs=[
            pl.BlockSpec((tm, qk_w), row),
            pl.BlockSpec((tm, qk_w), row),
            pl.BlockSpec((tm, v_w), row),
            pl.BlockSpec((tm, d_model), row),
            pl.BlockSpec((tm, d_model), row),
        ],
        out_shape=[
            jax.ShapeDtypeStruct((t, qk_w), BF16),
            jax.ShapeDtypeStruct((t, qk_w), BF16),
            jax.ShapeDtypeStruct((t, v_w), BF16),
            jax.ShapeDtypeStruct((t, d_model), BF16),
            jax.ShapeDtypeStruct((t, d_model), BF16),
        ],
        scratch_shapes=[
            pltpu.VMEM((POOL_HALO + tm, pool_w), F32),
            pltpu.VMEM((tm, pool_w), BF16),
        ],
        compiler_params=pltpu.CompilerParams(
            dimension_semantics=("arbitrary",), vmem_limit_bytes=VMEM_LIMIT_BYTES),
        name="proj_pool",
    )(x2, mix_pre_g, w_in, pool_mix, pool_scale, w_branch_a)


def _attn_kernel(slopes_ref, lq1_ref, lk1_ref, lq2_ref, lk2_ref, sg_ref,
                 q_ref, k_ref, v_ref, o_ref, kaug_ref, vaug_ref, acc1_ref, acc2_ref,
                 *, blk, lambda_init):
    h = pl.program_id(1)
    qi = pl.program_id(2)
    slope = slopes_ref[h]
    s_len = k_ref.shape[0]

    def hi_lo(idx):
        lo = idx & (BF16_EXACT_INT - 1)
        return (idx - lo).astype(F32), lo.astype(F32)

    @pl.when(qi == 0)
    def _():
        lane = lax.broadcasted_iota(jnp.int32, (s_len, AUG_LANES), 1)
        j = lax.broadcasted_iota(jnp.int32, (s_len, AUG_LANES), 0) & (blk - 1)
        jhi, jlo = hi_lo(j)
        kx = jnp.where(lane < 2, 1.0,
                       jnp.where(lane == 2, slope * jhi, jnp.where(lane == 3, slope * jlo, 0.0)))
        kaug_ref[:, 0:HEAD_WIDTH] = k_ref[...]
        kaug_ref[:, HEAD_WIDTH:] = kx.astype(BF16)
        vaug_ref[:, 0:HEAD_WIDTH] = v_ref[...]
        vaug_ref[:, HEAD_WIDTH:] = jnp.where(lane == 0, 1.0, 0.0).astype(BF16)

    lam = (jnp.exp(jnp.sum(lq1_ref[...] * lk1_ref[...], axis=-1, keepdims=True))
           - jnp.exp(jnp.sum(lq2_ref[...] * lk2_ref[...], axis=-1, keepdims=True))
           + lambda_init)

    q = q_ref[...]
    lane = lax.broadcasted_iota(jnp.int32, q.shape, 1)
    ihi, ilo = hi_lo(lax.broadcasted_iota(jnp.int32, q.shape, 0))
    qx = jnp.where(lane == 0, -slope * ihi,
                   jnp.where(lane == 1, -slope * ilo,
                             jnp.where(lane < 4, 1.0, 0.0))).astype(BF16)
    zero = jnp.zeros_like(q)
    q1 = jnp.concatenate([jnp.where(lane < HEAD_DIM, q, zero), qx], axis=1)
    q2 = jnp.concatenate([jnp.where(lane >= HEAD_DIM, q, zero), qx], axis=1)

    acc1_ref[...] = jnp.zeros_like(acc1_ref)
    acc2_ref[...] = jnp.zeros_like(acc2_ref)

    def softmax_step(s, c, m, acc_ref, vb):
        m_new = jnp.maximum(m, jnp.max(s, axis=-1, keepdims=True) + c)
        alpha = jnp.exp(m - m_new)
        p = jnp.exp(s - (m_new - c))
        acc_ref[...] = alpha * acc_ref[...] + jnp.dot(p.astype(BF16), vb,
                                                      preferred_element_type=F32)
        return m_new

    def block(kj, carry, masked):
        m1, m2 = carry
        start = pl.multiple_of(kj * blk, blk)
        kb = kaug_ref[pl.ds(start, blk), :]
        vb = vaug_ref[pl.ds(start, blk), :]
        nt = (((1,), (1,)), ((), ()))
        s1 = lax.dot_general(q1, kb, nt, preferred_element_type=F32)
        s2 = lax.dot_general(q2, kb, nt, preferred_element_type=F32)
        if masked:
            causal = (lax.broadcasted_iota(jnp.int32, (blk, blk), 0)
                      >= lax.broadcasted_iota(jnp.int32, (blk, blk), 1))
            s1 = jnp.where(causal, s1, MASK_VALUE)
            s2 = jnp.where(causal, s2, MASK_VALUE)
        c = -slope * ((qi - kj) * blk).astype(F32)
        return softmax_step(s1, c, m1, acc1_ref, vb), softmax_step(s2, c, m2, acc2_ref, vb)

    m0 = jnp.full((blk, 1), MASK_VALUE, F32)
    carry = lax.fori_loop(0, qi, lambda kj, cr: block(kj, cr, False), (m0, m0))
    block(qi, carry, True)

    a1 = acc1_ref[...]
    a2 = acc2_ref[...]
    o = (a1[:, :HEAD_WIDTH] / a1[:, HEAD_WIDTH:HEAD_WIDTH + 1]
         - lam * (a2[:, :HEAD_WIDTH] / a2[:, HEAD_WIDTH:HEAD_WIDTH + 1]))
    o_ref[...] = (_rms(o, sg_ref[...]) * (1.0 - lambda_init)).astype(BF16)


def _attn_call(q, k, v, slopes, lq1, lk1, lq2, lk2, subln_g, *, blk, lambda_init):
    b, s, width = q.shape
    heads = width // HEAD_WIDTH
    small = lambda bi, hi, qi: (0, 0)
    kern = functools.partial(_attn_kernel, blk=blk, lambda_init=lambda_init)
    return pl.pallas_call(
        kern,
        grid=(b, heads, s // blk),
        in_specs=[
            pl.BlockSpec(memory_space=pltpu.SMEM),
            pl.BlockSpec((1, HEAD_DIM), small),
            pl.BlockSpec((1, HEAD_DIM), small),
            pl.BlockSpec((1, HEAD_DIM), small),
            pl.BlockSpec((1, HEAD_DIM), small),
            pl.BlockSpec((1, HEAD_WIDTH), small),
            pl.BlockSpec((None, blk, HEAD_WIDTH), lambda bi, hi, qi: (bi, qi, hi)),
            pl.BlockSpec((None, s, HEAD_WIDTH), lambda bi, hi, qi: (bi, 0, hi)),
            pl.BlockSpec((None, s, HEAD_WIDTH), lambda bi, hi, qi: (bi, 0, hi)),
        ],
        out_specs=pl.BlockSpec((None, blk, HEAD_WIDTH), lambda bi, hi, qi: (bi, qi, hi)),
        out_shape=jax.ShapeDtypeStruct((b, s, width), BF16),
        scratch_shapes=[
            pltpu.VMEM((s, HEAD_WIDTH + AUG_LANES), BF16),
            pltpu.VMEM((s, HEAD_WIDTH + AUG_LANES), BF16),
            pltpu.VMEM((blk, HEAD_WIDTH + AUG_LANES), F32),
            pltpu.VMEM((blk, HEAD_WIDTH + AUG_LANES), F32),
        ],
        compiler_params=pltpu.CompilerParams(
            dimension_semantics=("parallel", "parallel", "arbitrary"),
            vmem_limit_bytes=VMEM_LIMIT_BYTES),
        name="diff_attn",
    )(slopes, lq1, lk1, lq2, lk2, subln_g, q, k, v)


def _out_ffn_kernel(x_ref, o_ref, ma_ref, sgb_ref, wb_ref, wo_ref, g_post_ref,
                    g_pre_ref, g_fpost_ref, wg_ref, wu_ref, wd_ref, out_ref):
    yb = jnp.dot(o_ref[...], wb_ref[...], preferred_element_type=F32)
    m = ma_ref[...].astype(F32) + sgb_ref[...].astype(F32) * yb
    mo = jnp.dot(m.astype(BF16), wo_ref[...], preferred_element_type=F32)
    h1 = x_ref[...] + _rms(mo, g_post_ref[...])
    u = _rms(h1, g_pre_ref[...]).astype(BF16)
    gate = jnp.dot(u, wg_ref[...], preferred_element_type=F32)
    up = jnp.dot(u, wu_ref[...], preferred_element_type=F32)
    f = (jax.nn.silu(gate) * up).astype(BF16)
    dn = jnp.dot(f, wd_ref[...], preferred_element_type=F32)
    out_ref[...] = h1 + _rms(dn, g_fpost_ref[...])


def _out_ffn_call(x2, o2, ma, sgb, w_branch_b, w_out, mix_post_g, ffn_pre_g, ffn_post_g,
                  w_gate, w_up, w_down, *, tm):
    t, d_model = x2.shape
    const = lambda i: (0, 0)
    row = lambda i: (i, 0)
    single = pl.Buffered(1)

    def wspec(w):
        return pl.BlockSpec(w.shape, const, pipeline_mode=single)

    return pl.pallas_call(
        _out_ffn_kernel,
        grid=(t // tm,),
        in_specs=[
            pl.BlockSpec((tm, d_model), row),
            pl.BlockSpec((tm, o2.shape[1]), row),
            pl.BlockSpec((tm, d_model), row),
            pl.BlockSpec((tm, d_model), row),
            wspec(w_branch_b),
            wspec(w_out),
            pl.BlockSpec((1, d_model), const),
            pl.BlockSpec((1, d_model), const),
            pl.BlockSpec((1, d_model), const),
            wspec(w_gate),
            wspec(w_up),
            wspec(w_down),
        ],
        out_specs=pl.BlockSpec((tm, d_model), row),
        out_shape=jax.ShapeDtypeStruct((t, d_model), F32),
        compiler_params=pltpu.CompilerParams(
            dimension_semantics=("parallel",), vmem_limit_bytes=VMEM_LIMIT_BYTES),
        name="out_ffn",
    )(x2, o2, ma, sgb, w_branch_b, w_out, mix_post_g, ffn_pre_g, ffn_post_g,
      w_gate, w_up, w_down)


def _alibi_slopes(n):
    start = 2.0 ** (-8.0 / n)
    return jnp.asarray([start ** (i + 1) for i in range(n)], dtype=F32)


def _layer(h, layer_idx, w_in, pool_mix, pool_scale, w_branch_a, lam_q1, lam_k1, lam_q2,
           lam_k2, subln_g, w_branch_b, w_out, mix_pre_g, mix_post_g, ffn_pre_g, ffn_post_g,
           w_ffn_gate, w_ffn_up, w_ffn_down):
    b, s, d_model = h.shape
    lambda_init = 0.8 - 0.6 * math.exp(-0.3 * layer_idx)
    x2 = h.reshape(b * s, d_model)
    row = lambda a: a.reshape(1, -1)

    q, k, v, ma, sgb = _proj_call(
        x2, row(mix_pre_g), w_in.astype(BF16), pool_mix.astype(BF16), row(pool_scale),
        w_branch_a.astype(BF16), seq=s, tm=512)

    width = q.shape[1]
    heads = width // HEAD_WIDTH
    o = _attn_call(q.reshape(b, s, width), k.reshape(b, s, width), v.reshape(b, s, width),
                   _alibi_slopes(heads), row(lam_q1), row(lam_k1), row(lam_q2), row(lam_k2),
                   row(subln_g), blk=1024, lambda_init=lambda_init)

    out = _out_ffn_call(
        x2, o.reshape(b * s, width), ma, sgb, w_branch_b.astype(BF16), w_out.astype(BF16),
        row(mix_post_g), row(ffn_pre_g), row(ffn_post_g), w_ffn_gate.astype(BF16),
        w_ffn_up.astype(BF16), w_ffn_down.astype(BF16), tm=256)
    return out.reshape(b, s, d_model)


def kernel(x, w_in, pool_mix, pool_scale, w_branch_a, lam_q1, lam_k1, lam_q2, lam_k2,
           subln_g, w_branch_b, w_out, mix_pre_g, mix_post_g, ffn_pre_g, ffn_post_g,
           w_ffn_gate, w_ffn_up, w_ffn_down):
    h = x
    for l in range(w_in.shape[0]):
        h = _layer(h, l, w_in[l], pool_mix[l], pool_scale[l], w_branch_a[l], lam_q1[l],
                   lam_k1[l], lam_q2[l], lam_k2[l], subln_g[l], w_branch_b[l], w_out[l],
                   mix_pre_g[l], mix_post_g[l], ffn_pre_g[l], ffn_post_g[l], w_ffn_gate[l],
                   w_ffn_up[l], w_ffn_down[l])
    return h
```

```python
import functools
import math

import jax
import jax.numpy as jnp
from jax import lax
from jax.experimental import pallas as pl
from jax.experimental.pallas import tpu as pltpu

NORM_EPS = 1e-6
POOL_WINDOWS = (2, 4, 8, 16)
POOL_GROUP_DIM = 128
POOL_HALO = 16
HEAD_DIM = 64
HEAD_WIDTH = 2 * HEAD_DIM
MASK_VALUE = -1e30
AUG_LANES = 128
BF16_EXACT_INT = 256
N_BIAS_LANES = 6
LOG2E = math.log2(math.e)
BF16 = jnp.bfloat16
F32 = jnp.float32

VMEM_LIMIT_BYTES = 52 * 1024 * 1024


def _rms(x, g):
    ms = jnp.mean(x * x, axis=-1, keepdims=True)
    return x * lax.rsqrt(ms + NORM_EPS) * g


def _proj_kernel(x_ref, g_ref, w_ref, pm_ref, ps_ref, wa_ref,
                 q_ref, k_ref, v_ref, ma_ref, sgb_ref,
                 pbuf_ref, ybuf_ref, *, tm, sub, tiles_per_seq, pool_w, qk_w, v_w, d_model):
    i = pl.program_id(0)
    seq_tile = i % tiles_per_seq
    tiles = [slice(r0, r0 + sub) for r0 in range(0, tm, sub)]
    us = [_rms(x_ref[rows, :], g_ref[...]).astype(BF16) for rows in tiles]

    def proj(u, lo, width):
        return jnp.dot(u, w_ref[:, lo:lo + width], preferred_element_type=F32)

    o1 = pool_w
    o2 = o1 + qk_w
    o3 = o2 + qk_w
    o4 = o3 + v_w
    o5 = o4 + d_model
    @pl.when(seq_tile == 0)
    def _():
        pbuf_ref[0:POOL_HALO, :] = jnp.zeros((POOL_HALO, pool_w), F32)

    @pl.when(seq_tile != 0)
    def _():
        pbuf_ref[0:POOL_HALO, :] = pbuf_ref[tm:tm + POOL_HALO, :]

    for rows, u in zip(tiles, us):
        pbuf_ref[POOL_HALO + rows.start:POOL_HALO + rows.stop, :] = proj(u, 0, pool_w)
    for rows, u in zip(tiles, us):
        q_ref[rows, :] = (proj(u, o1, qk_w) * (LOG2E / math.sqrt(HEAD_DIM))).astype(BF16)
        k_ref[rows, :] = proj(u, o2, qk_w).astype(BF16)

    t_loc = seq_tile * tm + lax.broadcasted_iota(jnp.int32, (tm, 1), 0)
    for g, w in enumerate(POOL_WINDOWS):
        cols = slice(g * POOL_GROUP_DIM, (g + 1) * POOL_GROUP_DIM)
        run = pbuf_ref[:, cols]
        d = 1
        while d < w:
            run = run + pltpu.roll(run, d, 0)
            d *= 2
        tok = pbuf_ref[POOL_HALO:POOL_HALO + tm, cols]
        cnt = jnp.minimum(t_loc + 1, w).astype(F32)
        pooled = run[POOL_HALO:, :] / cnt - tok
        y = jnp.dot(pooled.astype(BF16), pm_ref[g], preferred_element_type=F32)
        ybuf_ref[:, cols] = (y * ps_ref[:, cols]).astype(BF16)

    for rows, u in zip(tiles, us):
        v_ref[rows, :] = proj(u, o3, v_w).astype(BF16)
        sgb_ref[rows, :] = jax.nn.sigmoid(proj(u, o5, d_model)).astype(BF16)
    for rows, u in zip(tiles, us):
        ya = jnp.dot(ybuf_ref[rows, :], wa_ref[...], preferred_element_type=F32)
        ma_ref[rows, :] = (jax.nn.sigmoid(proj(u, o4, d_model)) * ya).astype(BF16)


def _proj_call(x2, mix_pre_g, w_in, pool_mix, pool_scale, w_branch_a, *, seq, tm, sub):
    t, d_model = x2.shape
    in_w = w_in.shape[1]
    pool_w = pool_scale.shape[1]
    v_w = pool_w
    qk_w = (in_w - pool_w - v_w - 2 * d_model) // 2
    const = lambda i: (0, 0)
    row = lambda i: (i, 0)
    single = pl.Buffered(1)
    kern = functools.partial(_proj_kernel, tm=tm, sub=sub, tiles_per_seq=seq // tm, pool_w=pool_w,
                             qk_w=qk_w, v_w=v_w, d_model=d_model)
    return pl.pallas_call(
        kern,
        grid=(t // tm,),
        in_specs=[
            pl.BlockSpec((tm, d_model), row),
            pl.BlockSpec((1, d_model), const),
            pl.BlockSpec((d_model, in_w), const, pipeline_mode=single),
            pl.BlockSpec(pool_mix.shape, lambda i: (0, 0, 0), pipeline_mode=single),
            pl.BlockSpec((1, pool_w), const),
            pl.BlockSpec((pool_w, d_model), const, pipeline_mode=single),
        ],
        out_specs=[
            pl.BlockSpec((tm, qk_w), row),
            pl.BlockSpec((tm, qk_w), row),
            pl.BlockSpec((tm, v_w), row),
            pl.BlockSpec((tm, d_model), row),
            pl.BlockSpec((tm, d_model), row),
        ],
        out_shape=[
            jax.ShapeDtypeStruct((t, qk_w), BF16),
            jax.ShapeDtypeStruct((t, qk_w), BF16),
            jax.ShapeDtypeStruct((t, v_w), BF16),
            jax.ShapeDtypeStruct((t, d_model), BF16),
            jax.ShapeDtypeStruct((t, d_model), BF16),
        ],
        scratch_shapes=[
            pltpu.VMEM((POOL_HALO + tm, pool_w), F32),
            pltpu.VMEM((tm, pool_w), BF16),
        ],
        compiler_params=pltpu.CompilerParams(
            dimension_semantics=("arbitrary",), vmem_limit_bytes=VMEM_LIMIT_BYTES),
        name="proj_pool",
    )(x2, mix_pre_g, w_in, pool_mix, pool_scale, w_branch_a)


def _attn_kernel(slopes_ref, lq1_ref, lk1_ref, lq2_ref, lk2_ref, sg_ref,
                 q_ref, k_ref, v_ref, o_ref, kaug_ref, vaug_ref, acc1_ref, acc2_ref,
                 *, blk, rc, lambda_init):
    h = pl.program_id(1)
    qi = pl.program_id(2)
    slope = slopes_ref[h]
    s_len = k_ref.shape[0]

    rate = slope * LOG2E
    lane_row = lax.broadcasted_iota(jnp.int32, (1, AUG_LANES), 1)
    part = jnp.where(lane_row >= N_BIAS_LANES, lane_row - N_BIAS_LANES, lane_row) >> 1
    r0 = jnp.full((1, AUG_LANES), rate, F32)
    r_hi = r0.astype(BF16).astype(F32)
    r_mid = (r0 - r_hi).astype(BF16).astype(F32)
    r_lo = (r0 - r_hi - r_mid).astype(BF16).astype(F32)
    rate_lanes = jnp.where(part == 0, r_hi, jnp.where(part == 1, r_mid, r_lo))

    def aug_lanes(idx, lane, negate_pos):
        lo = idx & (BF16_EXACT_INT - 1)
        pos = jnp.where((lane & 1) == 0, idx - lo, lo).astype(F32)
        first = lane < N_BIAS_LANES
        second = jnp.logical_and(lane >= N_BIAS_LANES, lane < 2 * N_BIAS_LANES)
        if negate_pos:
            x = jnp.where(first, -pos, jnp.where(second, rate_lanes, 0.0))
        else:
            x = jnp.where(first, rate_lanes, jnp.where(second, pos, 0.0))
        return x.astype(BF16)

    @pl.when(qi == 0)
    def _():
        lane = lax.broadcasted_iota(jnp.int32, (s_len, AUG_LANES), 1)
        j = lax.broadcasted_iota(jnp.int32, (s_len, AUG_LANES), 0) & (blk - 1)
        kaug_ref[:, 0:HEAD_WIDTH] = k_ref[...]
        kaug_ref[:, HEAD_WIDTH:] = aug_lanes(j, lane, False)
        vaug_ref[:, 0:HEAD_WIDTH] = v_ref[...]
        vaug_ref[:, HEAD_WIDTH:] = jnp.where(lane == 0, 1.0, 0.0).astype(BF16)

    lam = (jnp.exp(jnp.sum(lq1_ref[...] * lk1_ref[...], axis=-1, keepdims=True))
           - jnp.exp(jnp.sum(lq2_ref[...] * lk2_ref[...], axis=-1, keepdims=True))
           + lambda_init)

    q = q_ref[...]
    lane = lax.broadcasted_iota(jnp.int32, q.shape, 1)
    qx = aug_lanes(lax.broadcasted_iota(jnp.int32, q.shape, 0), lane, True)
    zero = jnp.zeros_like(q)
    q1 = jnp.concatenate([jnp.where(lane < HEAD_DIM, q, zero), qx], axis=1)
    q2 = jnp.concatenate([jnp.where(lane >= HEAD_DIM, q, zero), qx], axis=1)

    acc1_ref[...] = jnp.zeros_like(acc1_ref)
    acc2_ref[...] = jnp.zeros_like(acc2_ref)

    maps = ((q1, acc1_ref), (q2, acc2_ref))

    def scores(qr, rows, kstart, nk, mask_row0=None):
        kb = kaug_ref[pl.ds(kstart, nk), :]
        nt = (((1,), (1,)), ((), ()))
        s = lax.dot_general(qr[rows, :], kb, nt, preferred_element_type=F32)
        if mask_row0 is not None:
            causal = (lax.broadcasted_iota(jnp.int32, s.shape, 0) + mask_row0
                      >= lax.broadcasted_iota(jnp.int32, s.shape, 1))
            s = jnp.where(causal, s, MASK_VALUE)
        return s

    def update(s, ar, rows, kstart, nk, m, c):
        vb = vaug_ref[pl.ds(kstart, nk), :]
        m_new = jnp.maximum(m, jnp.max(s, axis=-1, keepdims=True) + c)
        alpha = jnp.exp2(m - m_new)
        p = jnp.exp2(s - (m_new - c))
        ar[rows, :] = alpha * ar[rows, :] + jnp.dot(p.astype(BF16), vb,
                                                    preferred_element_type=F32)
        return m_new

    every = slice(None)

    def full_block(kj, ms):
        start = pl.multiple_of(kj * blk, blk)
        c = -rate * ((qi - kj) * blk).astype(F32)
        ss = [scores(qr, every, start, blk) for qr, _ in maps]
        return tuple(update(s, ar, every, start, blk, m, c)
                     for s, (_, ar), m in zip(ss, maps, ms))

    def block_pair(t, ms):
        kjs = (2 * t, 2 * t + 1)
        starts = [pl.multiple_of(kj * blk, blk) for kj in kjs]
        cs = [-rate * ((qi - kj) * blk).astype(F32) for kj in kjs]
        (q_1, a_1), (q_2, a_2) = maps
        m_1, m_2 = ms
        s1a = scores(q_1, every, starts[0], blk)
        s2a = scores(q_2, every, starts[0], blk)
        m_1 = update(s1a, a_1, every, starts[0], blk, m_1, cs[0])
        s1b = scores(q_1, every, starts[1], blk)
        m_2 = update(s2a, a_2, every, starts[0], blk, m_2, cs[0])
        s2b = scores(q_2, every, starts[1], blk)
        m_1 = update(s1b, a_1, every, starts[1], blk, m_1, cs[1])
        m_2 = update(s2b, a_2, every, starts[1], blk, m_2, cs[1])
        return m_1, m_2

    ms = tuple(jnp.full((blk, 1), MASK_VALUE, F32) for _ in maps)
    ms = lax.fori_loop(0, qi // 2, block_pair, ms)
    ms = lax.fori_loop(0, qi % 2, lambda _, cr: full_block(qi - 1, cr), ms)

    start = pl.multiple_of(qi * blk, blk)
    for r in range(blk // rc):
        rows = slice(r * rc, (r + 1) * rc)
        nk = (r + 1) * rc
        ss = [scores(qr, rows, start, nk, mask_row0=r * rc) for qr, _ in maps]
        for s, (_, ar), m in zip(ss, maps, ms):
            update(s, ar, rows, start, nk, m[rows], 0.0)

    a1 = acc1_ref[...]
    a2 = acc2_ref[...]
    o = (a1[:, :HEAD_WIDTH] / a1[:, HEAD_WIDTH:HEAD_WIDTH + 1]
         - lam * (a2[:, :HEAD_WIDTH] / a2[:, HEAD_WIDTH:HEAD_WIDTH + 1]))
    o_ref[...] = (_rms(o, sg_ref[...]) * (1.0 - lambda_init)).astype(BF16)


def _attn_call(q, k, v, slopes, lq1, lk1, lq2, lk2, subln_g, *, blk, rc, lambda_init):
    b, s, width = q.shape
    heads = width // HEAD_WIDTH
    small = lambda bi, hi, qi: (0, 0)
    kern = functools.partial(_attn_kernel, blk=blk, rc=rc, lambda_init=lambda_init)
    return pl.pallas_call(
        kern,
        grid=(b, heads, s // blk),
        in_specs=[
            pl.BlockSpec(memory_space=pltpu.SMEM),
            pl.BlockSpec((1, HEAD_DIM), small),
            pl.BlockSpec((1, HEAD_DIM), small),
            pl.BlockSpec((1, HEAD_DIM), small),
            pl.BlockSpec((1, HEAD_DIM), small),
            pl.BlockSpec((1, HEAD_WIDTH), small),
            pl.BlockSpec((None, blk, HEAD_WIDTH), lambda bi, hi, qi: (bi, qi, hi)),
            pl.BlockSpec((None, s, HEAD_WIDTH), lambda bi, hi, qi: (bi, 0, hi)),
            pl.BlockSpec((None, s, HEAD_WIDTH), lambda bi, hi, qi: (bi, 0, hi)),
        ],
        out_specs=pl.BlockSpec((None, blk, HEAD_WIDTH), lambda bi, hi, qi: (bi, qi, hi)),
        out_shape=jax.ShapeDtypeStruct((b, s, width), BF16),
        scratch_shapes=[
            pltpu.VMEM((s, HEAD_WIDTH + AUG_LANES), BF16),
            pltpu.VMEM((s, HEAD_WIDTH + AUG_LANES), BF16),
            pltpu.VMEM((blk, HEAD_WIDTH + AUG_LANES), F32),
            pltpu.VMEM((blk, HEAD_WIDTH + AUG_LANES), F32),
        ],
        compiler_params=pltpu.CompilerParams(
            dimension_semantics=("parallel", "parallel", "arbitrary"),
            vmem_limit_bytes=VMEM_LIMIT_BYTES),
        name="diff_attn",
    )(slopes, lq1, lk1, lq2, lk2, subln_g, q, k, v)


def _out_ffn_kernel(x_ref, o_ref, ma_ref, sgb_ref, wb_ref, wo_ref, g_post_ref,
                    g_pre_ref, g_fpost_ref, wg_ref, wu_ref, wd_ref, out_ref, *, sub):
    tiles = [slice(r0, r0 + sub) for r0 in range(0, x_ref.shape[0], sub)]

    def mix(rows):
        yb = jnp.dot(o_ref[rows, :], wb_ref[...], preferred_element_type=F32)
        m = ma_ref[rows, :].astype(F32) + sgb_ref[rows, :].astype(F32) * yb
        return jnp.dot(m.astype(BF16), wo_ref[...], preferred_element_type=F32)

    def norms(rows, mo):
        h1 = x_ref[rows, :] + _rms(mo, g_post_ref[...])
        return h1, _rms(h1, g_pre_ref[...]).astype(BF16)

    def hidden(u):
        gate = jnp.dot(u, wg_ref[...], preferred_element_type=F32)
        up = jnp.dot(u, wu_ref[...], preferred_element_type=F32)
        return (jax.nn.silu(gate) * up).astype(BF16)

    mos = [mix(rows) for rows in tiles]
    hus = [norms(rows, mo) for rows, mo in zip(tiles, mos)]
    fs = [hidden(u) for _, u in hus]
    dns = [jnp.dot(f, wd_ref[...], preferred_element_type=F32) for f in fs]
    for rows, (h1, _), dn in zip(tiles, hus, dns):
        out_ref[rows, :] = h1 + _rms(dn, g_fpost_ref[...])


def _out_ffn_call(x2, o2, ma, sgb, w_branch_b, w_out, mix_post_g, ffn_pre_g, ffn_post_g,
                  w_gate, w_up, w_down, *, tm, sub):
    t, d_model = x2.shape
    const = lambda i: (0, 0)
    row = lambda i: (i, 0)
    single = pl.Buffered(1)

    def wspec(w):
        return pl.BlockSpec(w.shape, const, pipeline_mode=single)

    return pl.pallas_call(
        functools.partial(_out_ffn_kernel, sub=sub),
        grid=(t // tm,),
        in_specs=[
            pl.BlockSpec((tm, d_model), row),
            pl.BlockSpec((tm, o2.shape[1]), row),
            pl.BlockSpec((tm, d_model), row),
            pl.BlockSpec((tm, d_model), row),
            wspec(w_branch_b),
            wspec(w_out),
            pl.BlockSpec((1, d_model), const),
            pl.BlockSpec((1, d_model), const),
            pl.BlockSpec((1, d_model), const),
            wspec(w_gate),
            wspec(w_up),
            wspec(w_down),
        ],
        out_specs=pl.BlockSpec((tm, d_model), row),
        out_shape=jax.ShapeDtypeStruct((t, d_model), F32),
        compiler_params=pltpu.CompilerParams(
            dimension_semantics=("parallel",), vmem_limit_bytes=VMEM_LIMIT_BYTES),
        name="out_ffn",
    )(x2, o2, ma, sgb, w_branch_b, w_out, mix_post_g, ffn_pre_g, ffn_post_g,
      w_gate, w_up, w_down)


def _alibi_slopes(n):
    start = 2.0 ** (-8.0 / n)
    return jnp.asarray([start ** (i + 1) for i in range(n)], dtype=F32)


def _layer(h, layer_idx, w_in, pool_mix, pool_scale, w_branch_a, lam_q1, lam_k1, lam_q2,
           lam_k2, subln_g, w_branch_b, w_out, mix_pre_g, mix_post_g, ffn_pre_g, ffn_post_g,
           w_ffn_gate, w_ffn_up, w_ffn_down):
    b, s, d_model = h.shape
    lambda_init = 0.8 - 0.6 * math.exp(-0.3 * layer_idx)
    x2 = h.reshape(b * s, d_model)
    row = lambda a: a.reshape(1, -1)

    q, k, v, ma, sgb = _proj_call(
        x2, row(mix_pre_g), w_in.astype(BF16), pool_mix.astype(BF16), row(pool_scale),
        w_branch_a.astype(BF16), seq=s, tm=512, sub=256)

    width = q.shape[1]
    heads = width // HEAD_WIDTH
    o = _attn_call(q.reshape(b, s, width), k.reshape(b, s, width), v.reshape(b, s, width),
                   _alibi_slopes(heads), row(lam_q1), row(lam_k1), row(lam_q2), row(lam_k2),
                   row(subln_g), blk=1024, rc=512, lambda_init=lambda_init)

    out = _out_ffn_call(
        x2, o.reshape(b * s, width), ma, sgb, w_branch_b.astype(BF16), w_out.astype(BF16),
        row(mix_post_g), row(ffn_pre_g), row(ffn_post_g), w_ffn_gate.astype(BF16),
        w_ffn_up.astype(BF16), w_ffn_down.astype(BF16), tm=512, sub=256)
    return out.reshape(b, s, d_model)


def kernel(x, w_in, pool_mix, pool_scale, w_branch_a, lam_q1, lam_k1, lam_q2, lam_k2,
           subln_g, w_branch_b, w_out, mix_pre_g, mix_post_g, ffn_pre_g, ffn_post_g,
           w_ffn_gate, w_ffn_up, w_ffn_down):
    h = x
    for l in range(w_in.shape[0]):
        h = _layer(h, l, w_in[l], pool_mix[l], pool_scale[l], w_branch_a[l], lam_q1[l],
                   lam_k1[l], lam_q2[l], lam_k2[l], subln_g[l], w_branch_b[l], w_out[l],
                   mix_pre_g[l], mix_post_g[l], ffn_pre_g[l], ffn_post_g[l], w_ffn_gate[l],
                   w_ffn_up[l], w_ffn_down[l])
    return h
```

```python
import functools
import math

import jax
import jax.numpy as jnp
from jax import lax
from jax.experimental import pallas as pl
from jax.experimental.pallas import tpu as pltpu

NORM_EPS = 1e-6
POOL_WINDOWS = (2, 4, 8, 16)
POOL_GROUP_DIM = 128
POOL_HALO = 16
HEAD_DIM = 64
HEAD_WIDTH = 2 * HEAD_DIM
MASK_VALUE = -1e30
AUG_LANES = 128
BF16_EXACT_INT = 256
N_BIAS_LANES = 6
LOG2E = math.log2(math.e)
BF16 = jnp.bfloat16
F32 = jnp.float32

VMEM_LIMIT_BYTES = 52 * 1024 * 1024


def _rms(x, g):
    ms = jnp.mean(x * x, axis=-1, keepdims=True)
    return x * lax.rsqrt(ms + NORM_EPS) * g


def _proj_kernel(x_ref, g_ref, w_ref, pm_ref, ps_ref, wa_ref,
                 q_ref, k_ref, v_ref, ma_ref, sgb_ref,
                 pbuf_ref, ybuf_ref, *, tm, sub, tiles_per_seq, pool_w, qk_w, v_w, d_model):
    i = pl.program_id(0)
    seq_tile = i % tiles_per_seq
    tiles = [slice(r0, r0 + sub) for r0 in range(0, tm, sub)]
    us = [_rms(x_ref[rows, :], g_ref[...]).astype(BF16) for rows in tiles]

    def proj(u, lo, width):
        return jnp.dot(u, w_ref[:, lo:lo + width], preferred_element_type=F32)

    o1 = pool_w
    o2 = o1 + qk_w
    o3 = o2 + qk_w
    o4 = o3 + v_w
    o5 = o4 + d_model
    @pl.when(seq_tile == 0)
    def _():
        pbuf_ref[0:POOL_HALO, :] = jnp.zeros((POOL_HALO, pool_w), F32)

    @pl.when(seq_tile != 0)
    def _():
        pbuf_ref[0:POOL_HALO, :] = pbuf_ref[tm:tm + POOL_HALO, :]

    for rows, u in zip(tiles, us):
        pbuf_ref[POOL_HALO + rows.start:POOL_HALO + rows.stop, :] = proj(u, 0, pool_w)
    for rows, u in zip(tiles, us):
        q_ref[rows, :] = (proj(u, o1, qk_w) * (LOG2E / math.sqrt(HEAD_DIM))).astype(BF16)
        k_ref[rows, :] = proj(u, o2, qk_w).astype(BF16)

    t_loc = seq_tile * tm + lax.broadcasted_iota(jnp.int32, (tm, 1), 0)
    for g, w in enumerate(POOL_WINDOWS):
        cols = slice(g * POOL_GROUP_DIM, (g + 1) * POOL_GROUP_DIM)
        run = pbuf_ref[:, cols]
        d = 1
        while d < w:
            run = run + pltpu.roll(run, d, 0)
            d *= 2
        tok = pbuf_ref[POOL_HALO:POOL_HALO + tm, cols]
        cnt = jnp.minimum(t_loc + 1, w).astype(F32)
        pooled = run[POOL_HALO:, :] / cnt - tok
        y = jnp.dot(pooled.astype(BF16), pm_ref[g], preferred_element_type=F32)
        ybuf_ref[:, cols] = (y * ps_ref[:, cols]).astype(BF16)

    for rows, u in zip(tiles, us):
        v_ref[rows, :] = proj(u, o3, v_w).astype(BF16)
        sgb_ref[rows, :] = jax.nn.sigmoid(proj(u, o5, d_model)).astype(BF16)
    for rows, u in zip(tiles, us):
        ya = jnp.dot(ybuf_ref[rows, :], wa_ref[...], preferred_element_type=F32)
        ma_ref[rows, :] = (jax.nn.sigmoid(proj(u, o4, d_model)) * ya).astype(BF16)


def _proj_call(x2, mix_pre_g, w_in, pool_mix, pool_scale, w_branch_a, *, seq, tm, sub):
    t, d_model = x2.shape
    in_w = w_in.shape[1]
    pool_w = pool_scale.shape[1]
    v_w = pool_w
    qk_w = (in_w - pool_w - v_w - 2 * d_model) // 2
    const = lambda i: (0, 0)
    row = lambda i: (i, 0)
    single = pl.Buffered(1)
    kern = functools.partial(_proj_kernel, tm=tm, sub=sub, tiles_per_seq=seq // tm, pool_w=pool_w,
                             qk_w=qk_w, v_w=v_w, d_model=d_model)
    return pl.pallas_call(
        kern,
        grid=(t // tm,),
        in_specs=[
            pl.BlockSpec((tm, d_model), row),
            pl.BlockSpec((1, d_model), const),
            pl.BlockSpec((d_model, in_w), const, pipeline_mode=single),
            pl.BlockSpec(pool_mix.shape, lambda i: (0, 0, 0), pipeline_mode=single),
            pl.BlockSpec((1, pool_w), const),
            pl.BlockSpec((pool_w, d_model), const, pipeline_mode=single),
        ],
        out_specs=[
            pl.BlockSpec((tm, qk_w), row),
            pl.BlockSpec((tm, qk_w), row),
            pl.BlockSpec((tm, v_w), row),
            pl.BlockSpec((tm, d_model), row),
            pl.BlockSpec((tm, d_model), row),
        ],
        out_shape=[
            jax.ShapeDtypeStruct((t, qk_w), BF16),
            jax.ShapeDtypeStruct((t, qk_w), BF16),
            jax.ShapeDtypeStruct((t, v_w), BF16),
            jax.ShapeDtypeStruct((t, d_model), BF16),
            jax.ShapeDtypeStruct((t, d_model), BF16),
        ],
        scratch_shapes=[
            pltpu.VMEM((POOL_HALO + tm, pool_w), F32),
            pltpu.VMEM((tm, pool_w), BF16),
        ],
        compiler_params=pltpu.CompilerParams(
            dimension_semantics=("arbitrary",), vmem_limit_bytes=VMEM_LIMIT_BYTES),
        name="proj_pool",
    )(x2, mix_pre_g, w_in, pool_mix, pool_scale, w_branch_a)


def _attn_kernel(slopes_ref, lq1_ref, lk1_ref, lq2_ref, lk2_ref, sg_ref,
                 q_ref, k_ref, v_ref, o_ref, kaug_ref, vaug_ref, acc1_ref, acc2_ref, qx_ref,
                 *, blk, rc, lambda_init):
    h = pl.program_id(1)
    qi = pl.program_id(2)
    slope = slopes_ref[h]
    s_len = k_ref.shape[0]

    rate = slope * LOG2E
    lane_row = lax.broadcasted_iota(jnp.int32, (1, AUG_LANES), 1)
    part = jnp.where(lane_row >= N_BIAS_LANES, lane_row - N_BIAS_LANES, lane_row) >> 1
    r0 = jnp.full((1, AUG_LANES), rate, F32)
    r_hi = r0.astype(BF16).astype(F32)
    r_mid = (r0 - r_hi).astype(BF16).astype(F32)
    r_lo = (r0 - r_hi - r_mid).astype(BF16).astype(F32)
    rate_lanes = jnp.where(part == 0, r_hi, jnp.where(part == 1, r_mid, r_lo))

    def aug_lanes(idx, lane, negate_pos):
        lo = idx & (BF16_EXACT_INT - 1)
        pos = jnp.where((lane & 1) == 0, idx - lo, lo).astype(F32)
        first = lane < N_BIAS_LANES
        second = jnp.logical_and(lane >= N_BIAS_LANES, lane < 2 * N_BIAS_LANES)
        if negate_pos:
            x = jnp.where(first, -pos, jnp.where(second, rate_lanes, 0.0))
        else:
            x = jnp.where(first, rate_lanes, jnp.where(second, pos, 0.0))
        return x.astype(BF16)

    @pl.when(qi == 0)
    def _():
        lane = lax.broadcasted_iota(jnp.int32, (s_len, AUG_LANES), 1)
        j = lax.broadcasted_iota(jnp.int32, (s_len, AUG_LANES), 0) & (blk - 1)
        kaug_ref[:, 0:HEAD_WIDTH] = k_ref[...]
        kaug_ref[:, HEAD_WIDTH:] = aug_lanes(j, lane, False)
        vaug_ref[:, 0:HEAD_WIDTH] = v_ref[...]
        vaug_ref[:, HEAD_WIDTH:] = jnp.where(lane == 0, 1.0, 0.0).astype(BF16)
        qlane = lax.broadcasted_iota(jnp.int32, (blk, AUG_LANES), 1)
        qx_ref[...] = aug_lanes(lax.broadcasted_iota(jnp.int32, (blk, AUG_LANES), 0), qlane, True)

    lam =(jnp.exp(jnp.sum(lq1_ref[...] * lk1_ref[...], axis=-1, keepdims=True))
           - jnp.exp(jnp.sum(lq2_ref[...] * lk2_ref[...], axis=-1, keepdims=True))
           + lambda_init)

    q = q_ref[...]
    lane = lax.broadcasted_iota(jnp.int32, q.shape, 1)
    qx = qx_ref[...]
    zero = jnp.zeros_like(q)
    q1 = jnp.concatenate([jnp.where(lane < HEAD_DIM, q, zero), qx], axis=1)
    q2 = jnp.concatenate([jnp.where(lane >= HEAD_DIM, q, zero), qx], axis=1)

    acc1_ref[...] = jnp.zeros_like(acc1_ref)
    acc2_ref[...] = jnp.zeros_like(acc2_ref)

    maps = ((q1, acc1_ref), (q2, acc2_ref))

    def scores(qr, rows, kstart, nk, mask_row0=None):
        kb = kaug_ref[pl.ds(kstart, nk), :]
        nt = (((1,), (1,)), ((), ()))
        s = lax.dot_general(qr[rows, :], kb, nt, preferred_element_type=F32)
        if mask_row0 is not None:
            causal = (lax.broadcasted_iota(jnp.int32, s.shape, 0) + mask_row0
                      >= lax.broadcasted_iota(jnp.int32, s.shape, 1))
            s = jnp.where(causal, s, MASK_VALUE)
        return s

    def update(s, ar, rows, kstart, nk, m, c):
        vb = vaug_ref[pl.ds(kstart, nk), :]
        m_new = jnp.maximum(m, jnp.max(s, axis=-1, keepdims=True) + c)
        alpha = jnp.exp2(m - m_new)
        p = jnp.exp2(s - (m_new - c))
        ar[rows, :] = alpha * ar[rows, :] + jnp.dot(p.astype(BF16), vb,
                                                    preferred_element_type=F32)
        return m_new

    every = slice(None)

    def full_block(kj, ms):
        start = pl.multiple_of(kj * blk, blk)
        c = -rate * ((qi - kj) * blk).astype(F32)
        ss = [scores(qr, every, start, blk) for qr, _ in maps]
        return tuple(update(s, ar, every, start, blk, m, c)
                     for s, (_, ar), m in zip(ss, maps, ms))

    def block_pair(t, ms):
        kjs = (2 * t, 2 * t + 1)
        starts = [pl.multiple_of(kj * blk, blk) for kj in kjs]
        cs = [-rate * ((qi - kj) * blk).astype(F32) for kj in kjs]
        (q_1, a_1), (q_2, a_2) = maps
        m_1, m_2 = ms
        s1a = scores(q_1, every, starts[0], blk)
        s2a = scores(q_2, every, starts[0], blk)
        m_1 = update(s1a, a_1, every, starts[0], blk, m_1, cs[0])
        s1b = scores(q_1, every, starts[1], blk)
        m_2 = update(s2a, a_2, every, starts[0], blk, m_2, cs[0])
        s2b = scores(q_2, every, starts[1], blk)
        m_1 = update(s1b, a_1, every, starts[1], blk, m_1, cs[1])
        m_2 = update(s2b, a_2, every, starts[1], blk, m_2, cs[1])
        return m_1, m_2

    n_loop = jnp.maximum(qi - 1, 0)
    ms = tuple(jnp.full((blk, 1), MASK_VALUE, F32) for _ in maps)
    ms = lax.fori_loop(0, n_loop // 2, block_pair, ms)
    ms = lax.fori_loop(0, n_loop % 2, lambda _, cr: full_block(n_loop - 1, cr), ms)

    def tail(ms, with_full_block):
        start = pl.multiple_of(qi * blk, blk)
        if with_full_block:
            prev = pl.multiple_of((qi - 1) * blk, blk)
            ss_prev = [scores(qr, every, prev, blk) for qr, _ in maps]
        chunks = [(slice(r * rc, (r + 1) * rc), (r + 1) * rc) for r in range(blk // rc)]
        ss = [[scores(qr, rows, start, nk, mask_row0=rows.start) for qr, _ in maps]
              for rows, nk in chunks]
        if with_full_block:
            ms = tuple(update(s, ar, every, prev, blk, m, -rate * blk)
                       for s, (_, ar), m in zip(ss_prev, maps, ms))
        for (rows, nk), s_pair in zip(chunks, ss):
            for s, (_, ar), m in zip(s_pair, maps, ms):
                update(s, ar, rows, start, nk, m[rows], 0.0)

    pl.when(qi > 0)(lambda: tail(ms, True))
    pl.when(qi == 0)(lambda: tail(ms, False))

    a1 = acc1_ref[...]
    a2 = acc2_ref[...]
    o = (a1[:, :HEAD_WIDTH] / a1[:, HEAD_WIDTH:HEAD_WIDTH + 1]
         - lam * (a2[:, :HEAD_WIDTH] / a2[:, HEAD_WIDTH:HEAD_WIDTH + 1]))
    o_ref[...] = (_rms(o, sg_ref[...]) * (1.0 - lambda_init)).astype(BF16)


def _attn_call(q, k, v, slopes, lq1, lk1, lq2, lk2, subln_g, *, blk, rc, lambda_init):
    b, s, width = q.shape
    heads = width // HEAD_WIDTH
    small = lambda bi, hi, qi: (0, 0)
    kern = functools.partial(_attn_kernel, blk=blk, rc=rc, lambda_init=lambda_init)
    return pl.pallas_call(
        kern,
        grid=(b, heads, s // blk),
        in_specs=[
            pl.BlockSpec(memory_space=pltpu.SMEM),
            pl.BlockSpec((1, HEAD_DIM), small),
            pl.BlockSpec((1, HEAD_DIM), small),
            pl.BlockSpec((1, HEAD_DIM), small),
            pl.BlockSpec((1, HEAD_DIM), small),
            pl.BlockSpec((1, HEAD_WIDTH), small),
            pl.BlockSpec((None, blk, HEAD_WIDTH), lambda bi, hi, qi: (bi, qi, hi)),
            pl.BlockSpec((None, s, HEAD_WIDTH), lambda bi, hi, qi: (bi, 0, hi)),
            pl.BlockSpec((None, s, HEAD_WIDTH), lambda bi, hi, qi: (bi, 0, hi)),
        ],
        out_specs=pl.BlockSpec((None, blk, HEAD_WIDTH), lambda bi, hi, qi: (bi, qi, hi)),
        out_shape=jax.ShapeDtypeStruct((b, s, width), BF16),
        scratch_shapes=[
            pltpu.VMEM((s, HEAD_WIDTH + AUG_LANES), BF16),
            pltpu.VMEM((s, HEAD_WIDTH + AUG_LANES), BF16),
            pltpu.VMEM((blk, HEAD_WIDTH + AUG_LANES), F32),
            pltpu.VMEM((blk, HEAD_WIDTH + AUG_LANES), F32),
            pltpu.VMEM((blk, AUG_LANES), BF16),
        ],
        compiler_params=pltpu.CompilerParams(
            dimension_semantics=("parallel", "parallel", "arbitrary"),
            vmem_limit_bytes=VMEM_LIMIT_BYTES),
        name="diff_attn",
    )(slopes, lq1, lk1, lq2, lk2, subln_g, q, k, v)


def _out_ffn_kernel(x_ref, o_ref, ma_ref, sgb_ref, wb_ref, wo_ref, g_post_ref,
                    g_pre_ref, g_fpost_ref, wg_ref, wu_ref, wd_ref, out_ref, *, sub):
    tiles = [slice(r0, r0 + sub) for r0 in range(0, x_ref.shape[0], sub)]

    def mix(rows):
        yb = jnp.dot(o_ref[rows, :], wb_ref[...], preferred_element_type=F32)
        m = ma_ref[rows, :].astype(F32) + sgb_ref[rows, :].astype(F32) * yb
        return jnp.dot(m.astype(BF16), wo_ref[...], preferred_element_type=F32)

    def norms(rows, mo):
        h1 = x_ref[rows, :] + _rms(mo, g_post_ref[...])
        return h1, _rms(h1, g_pre_ref[...]).astype(BF16)

    def hidden(u):
        gate = jnp.dot(u, wg_ref[...], preferred_element_type=F32)
        up = jnp.dot(u, wu_ref[...], preferred_element_type=F32)
        return (jax.nn.silu(gate) * up).astype(BF16)

    mos = [mix(rows) for rows in tiles]
    hus = [norms(rows, mo) for rows, mo in zip(tiles, mos)]
    fs = [hidden(u) for _, u in hus]
    dns = [jnp.dot(f, wd_ref[...], preferred_element_type=F32) for f in fs]
    for rows, (h1, _), dn in zip(tiles, hus, dns):
        out_ref[rows, :] = h1 + _rms(dn, g_fpost_ref[...])


def _out_ffn_call(x2, o2, ma, sgb, w_branch_b, w_out, mix_post_g, ffn_pre_g, ffn_post_g,
                  w_gate, w_up, w_down, *, tm, sub):
    t, d_model = x2.shape
    const = lambda i: (0, 0)
    row = lambda i: (i, 0)
    single = pl.Buffered(1)

    def wspec(w):
        return pl.BlockSpec(w.shape, const, pipeline_mode=single)

    return pl.pallas_call(
        functools.partial(_out_ffn_kernel, sub=sub),
        grid=(t // tm,),
        in_specs=[
            pl.BlockSpec((tm, d_model), row),
            pl.BlockSpec((tm, o2.shape[1]), row),
            pl.BlockSpec((tm, d_model), row),
            pl.BlockSpec((tm, d_model), row),
            wspec(w_branch_b),
            wspec(w_out),
            pl.BlockSpec((1, d_model), const),
            pl.BlockSpec((1, d_model), const),
            pl.BlockSpec((1, d_model), const),
            wspec(w_gate),
            wspec(w_up),
            wspec(w_down),
        ],
        out_specs=pl.BlockSpec((tm, d_model), row),
        out_shape=jax.ShapeDtypeStruct((t, d_model), F32),
        compiler_params=pltpu.CompilerParams(
            dimension_semantics=("parallel",), vmem_limit_bytes=VMEM_LIMIT_BYTES),
        name="out_ffn",
    )(x2, o2, ma, sgb, w_branch_b, w_out, mix_post_g, ffn_pre_g, ffn_post_g,
      w_gate, w_up, w_down)


def _alibi_slopes(n):
    start = 2.0 ** (-8.0 / n)
    return jnp.asarray([start ** (i + 1) for i in range(n)], dtype=F32)


def _layer(h, layer_idx, w_in, pool_mix, pool_scale, w_branch_a, lam_q1, lam_k1, lam_q2,
           lam_k2, subln_g, w_branch_b, w_out, mix_pre_g, mix_post_g, ffn_pre_g, ffn_post_g,
           w_ffn_gate, w_ffn_up, w_ffn_down):
    b, s, d_model = h.shape
    lambda_init = 0.8 - 0.6 * math.exp(-0.3 * layer_idx)
    x2 = h.reshape(b * s, d_model)
    row = lambda a: a.reshape(1, -1)

    q, k, v, ma, sgb = _proj_call(
        x2, row(mix_pre_g), w_in.astype(BF16), pool_mix.astype(BF16), row(pool_scale),
        w_branch_a.astype(BF16), seq=s, tm=512, sub=256)

    width = q.shape[1]
    heads = width // HEAD_WIDTH
    o = _attn_call(q.reshape(b, s, width), k.reshape(b, s, width), v.reshape(b, s, width),
                   _alibi_slopes(heads), row(lam_q1), row(lam_k1), row(lam_q2), row(lam_k2),
                   row(subln_g), blk=1024, rc=256, lambda_init=lambda_init)

    out = _out_ffn_call(
        x2, o.reshape(b * s, width), ma, sgb, w_branch_b.astype(BF16), w_out.astype(BF16),
        row(mix_post_g), row(ffn_pre_g), row(ffn_post_g), w_ffn_gate.astype(BF16),
        w_ffn_up.astype(BF16), w_ffn_down.astype(BF16), tm=512, sub=256)
    return out.reshape(b, s, d_model)


def kernel(x, w_in, pool_mix, pool_scale, w_branch_a, lam_q1, lam_k1, lam_q2, lam_k2,
           subln_g, w_branch_b, w_out, mix_pre_g, mix_post_g, ffn_pre_g, ffn_post_g,
           w_ffn_gate, w_ffn_up, w_ffn_down):
    h = x
    for l in range(w_in.shape[0]):
        h = _layer(h, l, w_in[l], pool_mix[l], pool_scale[l], w_branch_a[l], lam_q1[l],
                   lam_k1[l], lam_q2[l], lam_k2[l], subln_g[l], w_branch_b[l], w_out[l],
                   mix_pre_g[l], mix_post_g[l], ffn_pre_g[l], ffn_post_g[l], w_ffn_gate[l],
                   w_ffn_up[l], w_ffn_down[l])
    return h
```

```python
import functools
import math

import jax
import jax.numpy as jnp
from jax import lax
from jax.experimental import pallas as pl
from jax.experimental.pallas import tpu as pltpu

NORM_EPS = 1e-6
POOL_WINDOWS = (2, 4, 8, 16)
POOL_GROUP_DIM = 128
POOL_HALO = 16
HEAD_DIM = 64
HEAD_WIDTH = 2 * HEAD_DIM
MASK_VALUE = -1e30
AUG_LANES = 128
BF16_EXACT_INT = 256
N_BIAS_LANES = 6
LOG2E = math.log2(math.e)
BF16 = jnp.bfloat16
F32 = jnp.float32

VMEM_LIMIT_BYTES = 52 * 1024 * 1024


def _rms(x, g):
    ms = jnp.mean(x * x, axis=-1, keepdims=True)
    return x * lax.rsqrt(ms + NORM_EPS) * g


def _proj_kernel(x_ref, g_ref, w_ref, pm_ref, ps_ref, wa_ref,
                 q_ref, k_ref, v_ref, ma_ref, sgb_ref,
                 pbuf_ref, ybuf_ref, *, tm, sub, tiles_per_seq, pool_w, qk_w, v_w, d_model):
    i = pl.program_id(0)
    seq_tile = i % tiles_per_seq
    tiles = [slice(r0, r0 + sub) for r0 in range(0, tm, sub)]
    us = [_rms(x_ref[rows, :], g_ref[...]).astype(BF16) for rows in tiles]

    def proj(u, lo, width):
        return jnp.dot(u, w_ref[:, lo:lo + width], preferred_element_type=F32)

    o1 = pool_w
    o2 = o1 + qk_w
    o3 = o2 + qk_w
    o4 = o3 + v_w
    o5 = o4 + d_model
    @pl.when(seq_tile == 0)
    def _():
        pbuf_ref[0:POOL_HALO, :] = jnp.zeros((POOL_HALO, pool_w), F32)

    @pl.when(seq_tile != 0)
    def _():
        pbuf_ref[0:POOL_HALO, :] = pbuf_ref[tm:tm + POOL_HALO, :]

    for rows, u in zip(tiles, us):
        pbuf_ref[POOL_HALO + rows.start:POOL_HALO + rows.stop, :] = proj(u, 0, pool_w)
    for rows, u in zip(tiles, us):
        q_ref[rows, :] = (proj(u, o1, qk_w) * (LOG2E / math.sqrt(HEAD_DIM))).astype(BF16)
        k_ref[rows, :] = proj(u, o2, qk_w).astype(BF16)

    t_loc = seq_tile * tm + lax.broadcasted_iota(jnp.int32, (tm, 1), 0)
    for g, w in enumerate(POOL_WINDOWS):
        cols = slice(g * POOL_GROUP_DIM, (g + 1) * POOL_GROUP_DIM)
        run = pbuf_ref[:, cols]
        d = 1
        while d < w:
            run = run + pltpu.roll(run, d, 0)
            d *= 2
        tok = pbuf_ref[POOL_HALO:POOL_HALO + tm, cols]
        cnt = jnp.minimum(t_loc + 1, w).astype(F32)
        pooled = run[POOL_HALO:, :] / cnt - tok
        y = jnp.dot(pooled.astype(BF16), pm_ref[g], preferred_element_type=F32)
        ybuf_ref[:, cols] = (y * ps_ref[:, cols]).astype(BF16)

    for rows, u in zip(tiles, us):
        v_ref[rows, :] = proj(u, o3, v_w).astype(BF16)
        sgb_ref[rows, :] = jax.nn.sigmoid(proj(u, o5, d_model)).astype(BF16)
    for rows, u in zip(tiles, us):
        ya = jnp.dot(ybuf_ref[rows, :], wa_ref[...], preferred_element_type=F32)
        ma_ref[rows, :] = (jax.nn.sigmoid(proj(u, o4, d_model)) * ya).astype(BF16)


def _proj_call(x2, mix_pre_g, w_in, pool_mix, pool_scale, w_branch_a, *, seq, tm, sub):
    t, d_model = x2.shape
    in_w = w_in.shape[1]
    pool_w = pool_scale.shape[1]
    v_w = pool_w
    qk_w = (in_w - pool_w - v_w - 2 * d_model) // 2
    const = lambda i: (0, 0)
    row = lambda i: (i, 0)
    single = pl.Buffered(1)
    kern = functools.partial(_proj_kernel, tm=tm, sub=sub, tiles_per_seq=seq // tm, pool_w=pool_w,
                             qk_w=qk_w, v_w=v_w, d_model=d_model)
    return pl.pallas_call(
        kern,
        grid=(t // tm,),
        in_specs=[
            pl.BlockSpec((tm, d_model), row),
            pl.BlockSpec((1, d_model), const),
            pl.BlockSpec((d_model, in_w), const, pipeline_mode=single),
            pl.BlockSpec(pool_mix.shape, lambda i: (0, 0, 0), pipeline_mode=single),
            pl.BlockSpec((1, pool_w), const),
            pl.BlockSpec((pool_w, d_model), const, pipeline_mode=single),
        ],
        out_specs=[
            pl.BlockSpec((tm, qk_w), row),
            pl.BlockSpec((tm, qk_w), row),
            pl.BlockSpec((tm, v_w), row),
            pl.BlockSpec((tm, d_model), row),
            pl.BlockSpec((tm, d_model), row),
        ],
        out_shape=[
            jax.ShapeDtypeStruct((t, qk_w), BF16),
            jax.ShapeDtypeStruct((t, qk_w), BF16),
            jax.ShapeDtypeStruct((t, v_w), BF16),
            jax.ShapeDtypeStruct((t, d_model), BF16),
            jax.ShapeDtypeStruct((t, d_model), BF16),
        ],
        scratch_shapes=[
            pltpu.VMEM((POOL_HALO + tm, pool_w), F32),
            pltpu.VMEM((tm, pool_w), BF16),
        ],
        compiler_params=pltpu.CompilerParams(
            dimension_semantics=("arbitrary",), vmem_limit_bytes=VMEM_LIMIT_BYTES),
        name="proj_pool",
    )(x2, mix_pre_g, w_in, pool_mix, pool_scale, w_branch_a)


def _attn_kernel(slopes_ref, lq1_ref, lk1_ref, lq2_ref, lk2_ref, sg_ref,
                 q_ref, k_ref, v_ref, o_ref, kaug_ref, vaug_ref, acc1_ref, acc2_ref, qx_ref,
                 *, blk, rc, lambda_init):
    h = pl.program_id(1)
    qi = pl.program_id(2)
    slope = slopes_ref[h]
    s_len = k_ref.shape[0]

    rate = slope * LOG2E
    lane_row = lax.broadcasted_iota(jnp.int32, (1, AUG_LANES), 1)
    part = jnp.where(lane_row >= N_BIAS_LANES, lane_row - N_BIAS_LANES, lane_row) >> 1
    r0 = jnp.full((1, AUG_LANES), rate, F32)
    r_hi = r0.astype(BF16).astype(F32)
    r_mid = (r0 - r_hi).astype(BF16).astype(F32)
    r_lo = (r0 - r_hi - r_mid).astype(BF16).astype(F32)
    rate_lanes = jnp.where(part == 0, r_hi, jnp.where(part == 1, r_mid, r_lo))

    def aug_lanes(idx, lane, negate_pos):
        lo = idx & (BF16_EXACT_INT - 1)
        pos = jnp.where((lane & 1) == 0, idx - lo, lo).astype(F32)
        first = lane < N_BIAS_LANES
        second = jnp.logical_and(lane >= N_BIAS_LANES, lane < 2 * N_BIAS_LANES)
        if negate_pos:
            x = jnp.where(first, -pos, jnp.where(second, rate_lanes, 0.0))
        else:
            x = jnp.where(first, rate_lanes, jnp.where(second, pos, 0.0))
        return x.astype(BF16)

    @pl.when(qi == 0)
    def _():
        period = 2 * blk
        lane = lax.broadcasted_iota(jnp.int32, (period, AUG_LANES), 1)
        kx = aug_lanes(lax.broadcasted_iota(jnp.int32, (period, AUG_LANES), 0), lane, False)
        vx = jnp.where(lane == 0, 1.0, 0.0).astype(BF16)
        kaug_ref[:, 0:HEAD_WIDTH] = k_ref[...]
        vaug_ref[:, 0:HEAD_WIDTH] = v_ref[...]
        for r0 in range(0, s_len, period):
            kaug_ref[r0:r0 + period, HEAD_WIDTH:] = kx
            vaug_ref[r0:r0 + period, HEAD_WIDTH:] = vx
        qx_ref[...] = aug_lanes(lax.broadcasted_iota(jnp.int32, (blk, AUG_LANES), 0),
                                lax.broadcasted_iota(jnp.int32, (blk, AUG_LANES), 1), True)

    lam =(jnp.exp(jnp.sum(lq1_ref[...] * lk1_ref[...], axis=-1, keepdims=True))
           - jnp.exp(jnp.sum(lq2_ref[...] * lk2_ref[...], axis=-1, keepdims=True))
           + lambda_init)

    q = q_ref[...]
    lane = lax.broadcasted_iota(jnp.int32, q.shape, 1)
    qx = qx_ref[...]
    zero = jnp.zeros_like(q)
    q1 = jnp.concatenate([jnp.where(lane < HEAD_DIM, q, zero), qx], axis=1)
    q2 = jnp.concatenate([jnp.where(lane >= HEAD_DIM, q, zero), qx], axis=1)

    acc1_ref[...] = jnp.zeros_like(acc1_ref)
    acc2_ref[...] = jnp.zeros_like(acc2_ref)

    maps = ((q1, acc1_ref), (q2, acc2_ref))

    def scores(qr, rows, kstart, nk, mask_row0=None):
        kb = kaug_ref[pl.ds(kstart, nk), :]
        nt = (((1,), (1,)), ((), ()))
        s = lax.dot_general(qr[rows, :], kb, nt, preferred_element_type=F32)
        if mask_row0 is not None:
            causal = (lax.broadcasted_iota(jnp.int32, s.shape, 0) + mask_row0
                      >= lax.broadcasted_iota(jnp.int32, s.shape, 1))
            s = jnp.where(causal, s, MASK_VALUE)
        return s

    def update(s, ar, rows, kstart, nk, m, c):
        vb = vaug_ref[pl.ds(kstart, nk), :]
        m_new = jnp.maximum(m, jnp.max(s, axis=-1, keepdims=True) + c)
        alpha = jnp.exp2(m - m_new)
        p = jnp.exp2(s - (m_new - c))
        ar[rows, :] = alpha * ar[rows, :] + jnp.dot(p.astype(BF16), vb,
                                                    preferred_element_type=F32)
        return m_new

    every = slice(None)

    def bias_const(kj):
        return -rate * (qi * blk - (kj >> 1) * (2 * blk)).astype(F32)

    def block_pair(t, ms):
        start = pl.multiple_of(t * (2 * blk), 2 * blk)
        ss = [scores(qr, every, start, 2 * blk) for qr, _ in maps]
        return tuple(update(s, ar, every, start, 2 * blk, m, bias_const(2 * t))
                     for s, (_, ar), m in zip(ss, maps, ms))

    ms = tuple(jnp.full((blk, 1), MASK_VALUE, F32) for _ in maps)
    ms = lax.fori_loop(0, qi // 2, block_pair, ms)

    def tail(n_before):
        start = pl.multiple_of(qi * blk - n_before, blk)
        chunks = [(slice(r * rc, (r + 1) * rc), n_before + (r + 1) * rc)
                  for r in range(blk // rc)]
        ss = [[scores(qr, rows, start, nk, mask_row0=n_before + rows.start) for qr, _ in maps]
              for rows, nk in chunks]
        for (rows, nk), s_pair in zip(chunks, ss):
            for s, (_, ar), m in zip(s_pair, maps, ms):
                update(s, ar, rows, start, nk, m[rows], bias_const(qi))
            a1 = acc1_ref[rows, :]
            a2 = acc2_ref[rows, :]
            o = (a1[:, :HEAD_WIDTH] / a1[:, HEAD_WIDTH:HEAD_WIDTH + 1]
                 - lam * (a2[:, :HEAD_WIDTH] / a2[:, HEAD_WIDTH:HEAD_WIDTH + 1]))
            o_ref[rows, :] = (_rms(o, sg_ref[...]) * (1.0 - lambda_init)).astype(BF16)

    pl.when(qi % 2 == 1)(lambda: tail(blk))
    pl.when(qi % 2 == 0)(lambda: tail(0))


def _attn_call(q, k, v, slopes, lq1, lk1, lq2, lk2, subln_g, *, blk, rc, lambda_init):
    b, s, width = q.shape
    heads = width // HEAD_WIDTH
    small = lambda bi, hi, qi: (0, 0)
    kern = functools.partial(_attn_kernel, blk=blk, rc=rc, lambda_init=lambda_init)
    return pl.pallas_call(
        kern,
        grid=(b, heads, s // blk),
        in_specs=[
            pl.BlockSpec(memory_space=pltpu.SMEM),
            pl.BlockSpec((1, HEAD_DIM), small),
            pl.BlockSpec((1, HEAD_DIM), small),
            pl.BlockSpec((1, HEAD_DIM), small),
            pl.BlockSpec((1, HEAD_DIM), small),
            pl.BlockSpec((1, HEAD_WIDTH), small),
            pl.BlockSpec((None, blk, HEAD_WIDTH), lambda bi, hi, qi: (bi, qi, hi)),
            pl.BlockSpec((None, s, HEAD_WIDTH), lambda bi, hi, qi: (bi, 0, hi)),
            pl.BlockSpec((None, s, HEAD_WIDTH), lambda bi, hi, qi: (bi, 0, hi)),
        ],
        out_specs=pl.BlockSpec((None, blk, HEAD_WIDTH), lambda bi, hi, qi: (bi, qi, hi)),
        out_shape=jax.ShapeDtypeStruct((b, s, width), BF16),
        scratch_shapes=[
            pltpu.VMEM((s, HEAD_WIDTH + AUG_LANES), BF16),
            pltpu.VMEM((s, HEAD_WIDTH + AUG_LANES), BF16),
            pltpu.VMEM((blk, HEAD_WIDTH + AUG_LANES), F32),
            pltpu.VMEM((blk, HEAD_WIDTH + AUG_LANES), F32),
            pltpu.VMEM((blk, AUG_LANES), BF16),
        ],
        compiler_params=pltpu.CompilerParams(
            dimension_semantics=("parallel", "parallel", "arbitrary"),
            vmem_limit_bytes=VMEM_LIMIT_BYTES),
        name="diff_attn",
    )(slopes, lq1, lk1, lq2, lk2, subln_g, q, k, v)


def _out_ffn_kernel(x_ref, o_ref, ma_ref, sgb_ref, wb_ref, wo_ref, g_post_ref,
                    g_pre_ref, g_fpost_ref, wg_ref, wu_ref, wd_ref, out_ref, *, sub):
    tiles = [slice(r0, r0 + sub) for r0 in range(0, x_ref.shape[0], sub)]

    def mix(rows):
        yb = jnp.dot(o_ref[rows, :], wb_ref[...], preferred_element_type=F32)
        m = ma_ref[rows, :].astype(F32) + sgb_ref[rows, :].astype(F32) * yb
        return jnp.dot(m.astype(BF16), wo_ref[...], preferred_element_type=F32)

    def norms(rows, mo):
        h1 = x_ref[rows, :] + _rms(mo, g_post_ref[...])
        return h1, _rms(h1, g_pre_ref[...]).astype(BF16)

    def hidden(u):
        gate = jnp.dot(u, wg_ref[...], preferred_element_type=F32)
        up = jnp.dot(u, wu_ref[...], preferred_element_type=F32)
        return (jax.nn.silu(gate) * up).astype(BF16)

    mos = [mix(rows) for rows in tiles]
    hus = [norms(rows, mo) for rows, mo in zip(tiles, mos)]
    fs = [hidden(u) for _, u in hus]
    dns = [jnp.dot(f, wd_ref[...], preferred_element_type=F32) for f in fs]
    for rows, (h1, _), dn in zip(tiles, hus, dns):
        out_ref[rows, :] = h1 + _rms(dn, g_fpost_ref[...])


def _out_ffn_call(x2, o2, ma, sgb, w_branch_b, w_out, mix_post_g, ffn_pre_g, ffn_post_g,
                  w_gate, w_up, w_down, *, tm, sub):
    t, d_model = x2.shape
    const = lambda i: (0, 0)
    row = lambda i: (i, 0)
    single = pl.Buffered(1)

    def wspec(w):
        return pl.BlockSpec(w.shape, const, pipeline_mode=single)

    return pl.pallas_call(
        functools.partial(_out_ffn_kernel, sub=sub),
        grid=(t // tm,),
        in_specs=[
            pl.BlockSpec((tm, d_model), row),
            pl.BlockSpec((tm, o2.shape[1]), row),
            pl.BlockSpec((tm, d_model), row),
            pl.BlockSpec((tm, d_model), row),
            wspec(w_branch_b),
            wspec(w_out),
            pl.BlockSpec((1, d_model), const),
            pl.BlockSpec((1, d_model), const),
            pl.BlockSpec((1, d_model), const),
            wspec(w_gate),
            wspec(w_up),
            wspec(w_down),
        ],
        out_specs=pl.BlockSpec((tm, d_model), row),
        out_shape=jax.ShapeDtypeStruct((t, d_model), F32),
        compiler_params=pltpu.CompilerParams(
            dimension_semantics=("parallel",), vmem_limit_bytes=VMEM_LIMIT_BYTES),
        name="out_ffn",
    )(x2, o2, ma, sgb, w_branch_b, w_out, mix_post_g, ffn_pre_g, ffn_post_g,
      w_gate, w_up, w_down)


def _alibi_slopes(n):
    start = 2.0 ** (-8.0 / n)
    return jnp.asarray([start ** (i + 1) for i in range(n)], dtype=F32)


def _layer(h, layer_idx, w_in, pool_mix, pool_scale, w_branch_a, lam_q1, lam_k1, lam_q2,
           lam_k2, subln_g, w_branch_b, w_out, mix_pre_g, mix_post_g, ffn_pre_g, ffn_post_g,
           w_ffn_gate, w_ffn_up, w_ffn_down):
    b, s, d_model = h.shape
    lambda_init = 0.8 - 0.6 * math.exp(-0.3 * layer_idx)
    x2 = h.reshape(b * s, d_model)
    row = lambda a: a.reshape(1, -1)

    q, k, v, ma, sgb = _proj_call(
        x2, row(mix_pre_g), w_in.astype(BF16), pool_mix.astype(BF16), row(pool_scale),
        w_branch_a.astype(BF16), seq=s, tm=512, sub=256)

    width = q.shape[1]
    heads = width // HEAD_WIDTH
    o = _attn_call(q.reshape(b, s, width), k.reshape(b, s, width), v.reshape(b, s, width),
                   _alibi_slopes(heads), row(lam_q1), row(lam_k1), row(lam_q2), row(lam_k2),
                   row(subln_g), blk=1024, rc=256, lambda_init=lambda_init)

    out = _out_ffn_call(
        x2, o.reshape(b * s, width), ma, sgb, w_branch_b.astype(BF16), w_out.astype(BF16),
        row(mix_post_g), row(ffn_pre_g), row(ffn_post_g), w_ffn_gate.astype(BF16),
        w_ffn_up.astype(BF16), w_ffn_down.astype(BF16), tm=512, sub=256)
    return out.reshape(b, s, d_model)


def kernel(x, w_in, pool_mix, pool_scale, w_branch_a, lam_q1, lam_k1, lam_q2, lam_k2,
           subln_g, w_branch_b, w_out, mix_pre_g, mix_post_g, ffn_pre_g, ffn_post_g,
           w_ffn_gate, w_ffn_up, w_ffn_down):
    h = x
    for l in range(w_in.shape[0]):
        h = _layer(h, l, w_in[l], pool_mix[l], pool_scale[l], w_branch_a[l], lam_q1[l],
                   lam_k1[l], lam_q2[l], lam_k2[l], subln_g[l], w_branch_b[l], w_out[l],
                   mix_pre_g[l], mix_post_g[l], ffn_pre_g[l], ffn_post_g[l], w_ffn_gate[l],
                   w_ffn_up[l], w_ffn_down[l])
    return h
```

```python
import functools
import math

import jax
import jax.numpy as jnp
from jax import lax
from jax.experimental import pallas as pl
from jax.experimental.pallas import tpu as pltpu

NORM_EPS = 1e-6
POOL_WINDOWS = (2, 4, 8, 16)
POOL_GROUP_DIM = 128
POOL_HALO = 16
HEAD_DIM = 64
HEAD_WIDTH = 2 * HEAD_DIM
MASK_VALUE = -1e30
AUG_LANES = 128
BF16_EXACT_INT = 256
BF16_SUBLANES = 16
N_BIAS_LANES = 6
LOG2E = math.log2(math.e)
BF16 = jnp.bfloat16
F32 = jnp.float32

VMEM_LIMIT_BYTES = 52 * 1024 * 1024


def _rms(x, g):
    ms = jnp.mean(x * x, axis=-1, keepdims=True)
    return x * lax.rsqrt(ms + NORM_EPS) * g


def _proj_kernel(x_ref, g_ref, w_ref, pm_ref, ps_ref, wa_ref, *rest,
                 n_cast, tm, sub, tiles_per_seq, pool_w, qk_w, v_w, d_model):
    cast_in = rest[:n_cast]
    q_ref, k_ref, v_ref, ma_ref, sgb_ref = rest[n_cast:n_cast + 5]
    cast_out = rest[n_cast + 5:2 * n_cast + 5]
    pbuf_ref, ybuf_ref = rest[2 * n_cast + 5:]
    i = pl.program_id(0)
    seq_tile = i % tiles_per_seq
    for src, dst in zip(cast_in, cast_out):
        dst[...] = src[...].astype(BF16)
    tiles = [slice(r0, r0 + sub) for r0 in range(0, tm, sub)]
    us = [_rms(x_ref[rows, :], g_ref[...]).astype(BF16) for rows in tiles]

    def proj(u, lo, width):
        return jnp.dot(u, w_ref[:, lo:lo + width], preferred_element_type=F32)

    o1 = pool_w
    o2 = o1 + qk_w
    o3 = o2 + qk_w
    o4 = o3 + v_w
    o5 = o4 + d_model
    @pl.when(seq_tile == 0)
    def _():
        pbuf_ref[0:POOL_HALO, :] = jnp.zeros((POOL_HALO, pool_w), F32)

    @pl.when(seq_tile != 0)
    def _():
        pbuf_ref[0:POOL_HALO, :] = pbuf_ref[tm:tm + POOL_HALO, :]

    for rows, u in zip(tiles, us):
        pbuf_ref[POOL_HALO + rows.start:POOL_HALO + rows.stop, :] = proj(u, 0, pool_w)
    for rows, u in zip(tiles, us):
        q_ref[rows, :] = (proj(u, o1, qk_w) * (LOG2E / math.sqrt(HEAD_DIM))).astype(BF16)
        k_ref[rows, :] = proj(u, o2, qk_w).astype(BF16)

    t_loc = seq_tile * tm + lax.broadcasted_iota(jnp.int32, (tm, 1), 0)
    for g, w in enumerate(POOL_WINDOWS):
        cols = slice(g * POOL_GROUP_DIM, (g + 1) * POOL_GROUP_DIM)
        run = pbuf_ref[:, cols]
        d = 1
        while d < w:
            run = run + pltpu.roll(run, d, 0)
            d *= 2
        tok = pbuf_ref[POOL_HALO:POOL_HALO + tm, cols]
        cnt = jnp.minimum(t_loc + 1, w).astype(F32)
        pooled = run[POOL_HALO:, :] / cnt - tok
        y = jnp.dot(pooled.astype(BF16), pm_ref[g], preferred_element_type=F32)
        ybuf_ref[:, cols] = (y * ps_ref[:, cols]).astype(BF16)

    for rows, u in zip(tiles, us):
        v_ref[rows, :] = proj(u, o3, v_w).astype(BF16)
        sgb_ref[rows, :] = jax.nn.sigmoid(proj(u, o5, d_model)).astype(BF16)
    for rows, u in zip(tiles, us):
        ya = jnp.dot(ybuf_ref[rows, :], wa_ref[...], preferred_element_type=F32)
        ma_ref[rows, :] = (jax.nn.sigmoid(proj(u, o4, d_model)) * ya).astype(BF16)


def _proj_call(x2, mix_pre_g, w_in, pool_mix, pool_scale, w_branch_a, later_weights,
               *, seq, tm, sub):
    t, d_model = x2.shape
    in_w = w_in.shape[1]
    pool_w = pool_scale.shape[1]
    v_w = pool_w
    qk_w = (in_w - pool_w - v_w - 2 * d_model) // 2
    steps = t // tm
    const = lambda i: (0, 0)
    row = lambda i: (i, 0)
    single = pl.Buffered(1)
    slabs = [w.reshape(steps * BF16_SUBLANES, -1) for w in later_weights]
    slab_specs = [pl.BlockSpec((BF16_SUBLANES, w.shape[1]), row) for w in slabs]
    kern = functools.partial(_proj_kernel, n_cast=len(slabs), tm=tm, sub=sub,
                             tiles_per_seq=seq // tm, pool_w=pool_w, qk_w=qk_w, v_w=v_w,
                             d_model=d_model)
    outs = pl.pallas_call(
        kern,
        grid=(steps,),
        in_specs=[
            pl.BlockSpec((tm, d_model), row),
            pl.BlockSpec((1, d_model), const),
            pl.BlockSpec((d_model, in_w), const, pipeline_mode=single),
            pl.BlockSpec(pool_mix.shape, lambda i: (0, 0, 0), pipeline_mode=single),
            pl.BlockSpec((1, pool_w), const),
            pl.BlockSpec((pool_w, d_model), const, pipeline_mode=single),
        ] + slab_specs,
        out_specs=[
            pl.BlockSpec((tm, qk_w), row),
            pl.BlockSpec((tm, qk_w), row),
            pl.BlockSpec((tm, v_w), row),
            pl.BlockSpec((tm, d_model), row),
            pl.BlockSpec((tm, d_model), row),
        ] + slab_specs,
        out_shape=[
            jax.ShapeDtypeStruct((t, qk_w), BF16),
            jax.ShapeDtypeStruct((t, qk_w), BF16),
            jax.ShapeDtypeStruct((t, v_w), BF16),
            jax.ShapeDtypeStruct((t, d_model), BF16),
            jax.ShapeDtypeStruct((t, d_model), BF16),
        ] + [jax.ShapeDtypeStruct(w.shape, BF16) for w in slabs],
        scratch_shapes=[
            pltpu.VMEM((POOL_HALO + tm, pool_w), F32),
            pltpu.VMEM((tm, pool_w), BF16),
        ],
        compiler_params=pltpu.CompilerParams(
            dimension_semantics=("arbitrary",), vmem_limit_bytes=VMEM_LIMIT_BYTES),
        name="proj_pool",
    )(x2, mix_pre_g, w_in, pool_mix, pool_scale, w_branch_a, *slabs)
    cast = [c.reshape(w.shape) for c, w in zip(outs[5:], later_weights)]
    return outs[:5], cast


def _attn_kernel(slopes_ref, lq1_ref, lk1_ref, lq2_ref, lk2_ref, sg_ref,
                 q_ref, k_ref, v_ref, o_ref, kaug_ref, vaug_ref, acc1_ref, acc2_ref, qx_ref,
                 *, blk, rc, lambda_init):
    h = pl.program_id(1)
    qi = pl.program_id(2)
    slope = slopes_ref[h]
    s_len = k_ref.shape[0]

    rate = slope * LOG2E
    lane_row = lax.broadcasted_iota(jnp.int32, (1, AUG_LANES), 1)
    part = jnp.where(lane_row >= N_BIAS_LANES, lane_row - N_BIAS_LANES, lane_row) >> 1
    r0 = jnp.full((1, AUG_LANES), rate, F32)
    r_hi = r0.astype(BF16).astype(F32)
    r_mid = (r0 - r_hi).astype(BF16).astype(F32)
    r_lo = (r0 - r_hi - r_mid).astype(BF16).astype(F32)
    rate_lanes = jnp.where(part == 0, r_hi, jnp.where(part == 1, r_mid, r_lo))

    def aug_lanes(idx, lane, negate_pos):
        lo = idx & (BF16_EXACT_INT - 1)
        pos = jnp.where((lane & 1) == 0, idx - lo, lo).astype(F32)
        first = lane < N_BIAS_LANES
        second = jnp.logical_and(lane >= N_BIAS_LANES, lane < 2 * N_BIAS_LANES)
        if negate_pos:
            x = jnp.where(first, -pos, jnp.where(second, rate_lanes, 0.0))
        else:
            x = jnp.where(first, rate_lanes, jnp.where(second, pos, 0.0))
        return x.astype(BF16)

    @pl.when(qi == 0)
    def _():
        period = 2 * blk
        lane = lax.broadcasted_iota(jnp.int32, (period, AUG_LANES), 1)
        kx = aug_lanes(lax.broadcasted_iota(jnp.int32, (period, AUG_LANES), 0), lane, False)
        vx = jnp.where(lane == 0, 1.0, 0.0).astype(BF16)
        kaug_ref[:, 0:HEAD_WIDTH] = k_ref[...]
        vaug_ref[:, 0:HEAD_WIDTH] = v_ref[...]
        for r0 in range(0, s_len, period):
            kaug_ref[r0:r0 + period, HEAD_WIDTH:] = kx
            vaug_ref[r0:r0 + period, HEAD_WIDTH:] = vx
        qx_ref[...] = aug_lanes(lax.broadcasted_iota(jnp.int32, (blk, AUG_LANES), 0),
                                lax.broadcasted_iota(jnp.int32, (blk, AUG_LANES), 1), True)

    lam =(jnp.exp(jnp.sum(lq1_ref[...] * lk1_ref[...], axis=-1, keepdims=True))
           - jnp.exp(jnp.sum(lq2_ref[...] * lk2_ref[...], axis=-1, keepdims=True))
           + lambda_init)

    q = q_ref[...]
    lane = lax.broadcasted_iota(jnp.int32, q.shape, 1)
    qx = qx_ref[...]
    zero = jnp.zeros_like(q)
    q1 = jnp.concatenate([jnp.where(lane < HEAD_DIM, q, zero), qx], axis=1)
    q2 = jnp.concatenate([jnp.where(lane >= HEAD_DIM, q, zero), qx], axis=1)

    acc1_ref[...] = jnp.zeros_like(acc1_ref)
    acc2_ref[...] = jnp.zeros_like(acc2_ref)

    maps = ((q1, acc1_ref), (q2, acc2_ref))

    def scores(qr, rows, kstart, nk, mask_row0=None):
        kb = kaug_ref[pl.ds(kstart, nk), :]
        nt = (((1,), (1,)), ((), ()))
        s = lax.dot_general(qr[rows, :], kb, nt, preferred_element_type=F32)
        if mask_row0 is not None:
            causal = (lax.broadcasted_iota(jnp.int32, s.shape, 0) + mask_row0
                      >= lax.broadcasted_iota(jnp.int32, s.shape, 1))
            s = jnp.where(causal, s, MASK_VALUE)
        return s

    def update(s, ar, rows, kstart, nk, m, c):
        vb = vaug_ref[pl.ds(kstart, nk), :]
        m_new = jnp.maximum(m, jnp.max(s, axis=-1, keepdims=True) + c)
        alpha = jnp.exp2(m - m_new)
        p = jnp.exp2(s - (m_new - c))
        ar[rows, :] = alpha * ar[rows, :] + jnp.dot(p.astype(BF16), vb,
                                                    preferred_element_type=F32)
        return m_new

    every = slice(None)

    def bias_const(kj):
        return -rate * (qi * blk - (kj >> 1) * (2 * blk)).astype(F32)

    def block_pair(t, ms):
        start = pl.multiple_of(t * (2 * blk), 2 * blk)
        ss = [scores(qr, every, start, 2 * blk) for qr, _ in maps]
        return tuple(update(s, ar, every, start, 2 * blk, m, bias_const(2 * t))
                     for s, (_, ar), m in zip(ss, maps, ms))

    ms = tuple(jnp.full((blk, 1), MASK_VALUE, F32) for _ in maps)
    ms = lax.fori_loop(0, qi // 2, block_pair, ms)

    def tail(n_before):
        start = pl.multiple_of(qi * blk - n_before, blk)
        chunks = [(slice(r * rc, (r + 1) * rc), n_before + (r + 1) * rc)
                  for r in range(blk // rc)]
        ss = [[scores(qr, rows, start, nk, mask_row0=n_before + rows.start) for qr, _ in maps]
              for rows, nk in chunks]
        for (rows, nk), s_pair in zip(chunks, ss):
            for s, (_, ar), m in zip(s_pair, maps, ms):
                update(s, ar, rows, start, nk, m[rows], bias_const(qi))
            a1 = acc1_ref[rows, :]
            a2 = acc2_ref[rows, :]
            o = (a1[:, :HEAD_WIDTH] / a1[:, HEAD_WIDTH:HEAD_WIDTH + 1]
                 - lam * (a2[:, :HEAD_WIDTH] / a2[:, HEAD_WIDTH:HEAD_WIDTH + 1]))
            o_ref[rows, :] = (_rms(o, sg_ref[...]) * (1.0 - lambda_init)).astype(BF16)

    pl.when(qi % 2 == 1)(lambda: tail(blk))
    pl.when(qi % 2 == 0)(lambda: tail(0))


def _attn_call(q, k, v, slopes, lq1, lk1, lq2, lk2, subln_g, *, blk, rc, lambda_init):
    b, s, width = q.shape
    heads = width // HEAD_WIDTH
    small = lambda bi, hi, qi: (0, 0)
    kern = functools.partial(_attn_kernel, blk=blk, rc=rc, lambda_init=lambda_init)
    return pl.pallas_call(
        kern,
        grid=(b, heads, s // blk),
        in_specs=[
            pl.BlockSpec(memory_space=pltpu.SMEM),
            pl.BlockSpec((1, HEAD_DIM), small),
            pl.BlockSpec((1, HEAD_DIM), small),
            pl.BlockSpec((1, HEAD_DIM), small),
            pl.BlockSpec((1, HEAD_DIM), small),
            pl.BlockSpec((1, HEAD_WIDTH), small),
            pl.BlockSpec((None, blk, HEAD_WIDTH), lambda bi, hi, qi: (bi, qi, hi)),
            pl.BlockSpec((None, s, HEAD_WIDTH), lambda bi, hi, qi: (bi, 0, hi)),
            pl.BlockSpec((None, s, HEAD_WIDTH), lambda bi, hi, qi: (bi, 0, hi)),
        ],
        out_specs=pl.BlockSpec((None, blk, HEAD_WIDTH), lambda bi, hi, qi: (bi, qi, hi)),
        out_shape=jax.ShapeDtypeStruct((b, s, width), BF16),
        scratch_shapes=[
            pltpu.VMEM((s, HEAD_WIDTH + AUG_LANES), BF16),
            pltpu.VMEM((s, HEAD_WIDTH + AUG_LANES), BF16),
            pltpu.VMEM((blk, HEAD_WIDTH + AUG_LANES), F32),
            pltpu.VMEM((blk, HEAD_WIDTH + AUG_LANES), F32),
            pltpu.VMEM((blk, AUG_LANES), BF16),
        ],
        compiler_params=pltpu.CompilerParams(
            dimension_semantics=("parallel", "parallel", "arbitrary"),
            vmem_limit_bytes=VMEM_LIMIT_BYTES),
        name="diff_attn",
    )(slopes, lq1, lk1, lq2, lk2, subln_g, q, k, v)


def _out_ffn_kernel(x_ref, o_ref, ma_ref, sgb_ref, wb_ref, wo_ref, g_post_ref,
                    g_pre_ref, g_fpost_ref, wg_ref, wu_ref, wd_ref, out_ref, *, sub):
    tiles = [slice(r0, r0 + sub) for r0 in range(0, x_ref.shape[0], sub)]

    def mix(rows):
        yb = jnp.dot(o_ref[rows, :], wb_ref[...], preferred_element_type=F32)
        m = ma_ref[rows, :].astype(F32) + sgb_ref[rows, :].astype(F32) * yb
        return jnp.dot(m.astype(BF16), wo_ref[...], preferred_element_type=F32)

    def norms(rows, mo):
        h1 = x_ref[rows, :] + _rms(mo, g_post_ref[...])
        return h1, _rms(h1, g_pre_ref[...]).astype(BF16)

    def hidden(u):
        gate = jnp.dot(u, wg_ref[...], preferred_element_type=F32)
        up = jnp.dot(u, wu_ref[...], preferred_element_type=F32)
        return (jax.nn.silu(gate) * up).astype(BF16)

    mos = [mix(rows) for rows in tiles]
    hus = [norms(rows, mo) for rows, mo in zip(tiles, mos)]
    fs = [hidden(u) for _, u in hus]
    dns = [jnp.dot(f, wd_ref[...], preferred_element_type=F32) for f in fs]
    for rows, (h1, _), dn in zip(tiles, hus, dns):
        out_ref[rows, :] = h1 + _rms(dn, g_fpost_ref[...])


def _out_ffn_call(x2, o2, ma, sgb, w_branch_b, w_out, mix_post_g, ffn_pre_g, ffn_post_g,
                  w_gate, w_up, w_down, *, tm, sub):
    t, d_model = x2.shape
    const = lambda i: (0, 0)
    row = lambda i: (i, 0)
    single = pl.Buffered(1)

    def wspec(w):
        return pl.BlockSpec(w.shape, const, pipeline_mode=single)

    return pl.pallas_call(
        functools.partial(_out_ffn_kernel, sub=sub),
        grid=(t // tm,),
        in_specs=[
            pl.BlockSpec((tm, d_model), row),
            pl.BlockSpec((tm, o2.shape[1]), row),
            pl.BlockSpec((tm, d_model), row),
            pl.BlockSpec((tm, d_model), row),
            wspec(w_branch_b),
            wspec(w_out),
            pl.BlockSpec((1, d_model), const),
            pl.BlockSpec((1, d_model), const),
            pl.BlockSpec((1, d_model), const),
            wspec(w_gate),
            wspec(w_up),
            wspec(w_down),
        ],
        out_specs=pl.BlockSpec((tm, d_model), row),
        out_shape=jax.ShapeDtypeStruct((t, d_model), F32),
        compiler_params=pltpu.CompilerParams(
            dimension_semantics=("parallel",), vmem_limit_bytes=VMEM_LIMIT_BYTES),
        name="out_ffn",
    )(x2, o2, ma, sgb, w_branch_b, w_out, mix_post_g, ffn_pre_g, ffn_post_g,
      w_gate, w_up, w_down)


def _alibi_slopes(n):
    start = 2.0 ** (-8.0 / n)
    return jnp.asarray([start ** (i + 1) for i in range(n)], dtype=F32)


def _layer(h, layer_idx, w_in, pool_mix, pool_scale, w_branch_a, lam_q1, lam_k1, lam_q2,
           lam_k2, subln_g, w_branch_b, w_out, mix_pre_g, mix_post_g, ffn_pre_g, ffn_post_g,
           w_ffn_gate, w_ffn_up, w_ffn_down):
    b, s, d_model = h.shape
    lambda_init = 0.8 - 0.6 * math.exp(-0.3 * layer_idx)
    x2 = h.reshape(b * s, d_model)
    row = lambda a: a.reshape(1, -1)

    (q, k, v, ma, sgb), (wb16, wo16, wg16, wu16, wd16) = _proj_call(
        x2, row(mix_pre_g), w_in.astype(BF16), pool_mix.astype(BF16), row(pool_scale),
        w_branch_a.astype(BF16), (w_branch_b, w_out, w_ffn_gate, w_ffn_up, w_ffn_down),
        seq=s, tm=1024, sub=256)

    width = q.shape[1]
    heads = width // HEAD_WIDTH
    o = _attn_call(q.reshape(b, s, width), k.reshape(b, s, width), v.reshape(b, s, width),
                   _alibi_slopes(heads), row(lam_q1), row(lam_k1), row(lam_q2), row(lam_k2),
                   row(subln_g), blk=1024, rc=256, lambda_init=lambda_init)

    out = _out_ffn_call(
        x2, o.reshape(b * s, width), ma, sgb, wb16, wo16,
        row(mix_post_g), row(ffn_pre_g), row(ffn_post_g), wg16, wu16, wd16, tm=512, sub=256)
    return out.reshape(b, s, d_model)


def kernel(x, w_in, pool_mix, pool_scale, w_branch_a, lam_q1, lam_k1, lam_q2, lam_k2,
           subln_g, w_branch_b, w_out, mix_pre_g, mix_post_g, ffn_pre_g, ffn_post_g,
           w_ffn_gate, w_ffn_up, w_ffn_down):
    h = x
    for l in range(w_in.shape[0]):
        h = _layer(h, l, w_in[l], pool_mix[l], pool_scale[l], w_branch_a[l], lam_q1[l],
                   lam_k1[l], lam_q2[l], lam_k2[l], subln_g[l], w_branch_b[l], w_out[l],
                   mix_pre_g[l], mix_post_g[l], ffn_pre_g[l], ffn_post_g[l], w_ffn_gate[l],
                   w_ffn_up[l], w_ffn_down[l])
    return h
```

```python
import functools
import math

import jax
import jax.numpy as jnp
from jax import lax
from jax.experimental import pallas as pl
from jax.experimental.pallas import tpu as pltpu

NORM_EPS = 1e-6
POOL_WINDOWS = (2, 4, 8, 16)
POOL_GROUP_DIM = 128
POOL_HALO = 16
HEAD_DIM = 64
HEAD_WIDTH = 2 * HEAD_DIM
MASK_VALUE = -1e30
AUG_LANES = 128
BF16_EXACT_INT = 256
BF16_SUBLANES = 16
N_BIAS_LANES = 6
LOG2E = math.log2(math.e)
BF16 = jnp.bfloat16
F32 = jnp.float32

VMEM_LIMIT_BYTES = 52 * 1024 * 1024


def _rms(x, g):
    ms = jnp.mean(x * x, axis=-1, keepdims=True)
    return x * lax.rsqrt(ms + NORM_EPS) * g


def _proj_kernel(x_ref, g_ref, w_ref, pm_ref, ps_ref, wa_ref, *rest,
                 cast_steps, tm, sub, tiles_per_seq, pool_w, qk_w, v_w, d_model):
    n_cast = len(cast_steps)
    cast_in = rest[:n_cast]
    q_ref, k_ref, v_ref, ma_ref, sgb_ref = rest[n_cast:n_cast + 5]
    cast_out = rest[n_cast + 5:2 * n_cast + 5]
    pbuf_ref, ybuf_ref = rest[2 * n_cast + 5:]
    i = pl.program_id(0)
    seq_tile = i % tiles_per_seq
    for src, dst, n in zip(cast_in, cast_out, cast_steps):
        @pl.when(i < n)
        def _(src=src, dst=dst):
            dst[...] = src[...].astype(BF16)

    tiles = [slice(r0, r0 + sub) for r0 in range(0, tm, sub)]
    us = [_rms(x_ref[rows, :], g_ref[...]).astype(BF16) for rows in tiles]

    def proj(u, lo, width):
        return jnp.dot(u, w_ref[:, lo:lo + width], preferred_element_type=F32)

    o1 = pool_w
    o2 = o1 + qk_w
    o3 = o2 + qk_w
    o4 = o3 + v_w
    o5 = o4 + d_model
    @pl.when(seq_tile == 0)
    def _():
        pbuf_ref[0:POOL_HALO, :] = jnp.zeros((POOL_HALO, pool_w), F32)

    @pl.when(seq_tile != 0)
    def _():
        pbuf_ref[0:POOL_HALO, :] = pbuf_ref[tm:tm + POOL_HALO, :]

    for rows, u in zip(tiles, us):
        pbuf_ref[POOL_HALO + rows.start:POOL_HALO + rows.stop, :] = proj(u, 0, pool_w)
    for rows, u in zip(tiles, us):
        q_ref[rows, :] = (proj(u, o1, qk_w) * (LOG2E / math.sqrt(HEAD_DIM))).astype(BF16)
        k_ref[rows, :] = proj(u, o2, qk_w).astype(BF16)

    t_loc = seq_tile * tm + lax.broadcasted_iota(jnp.int32, (tm, 1), 0)
    for g, w in enumerate(POOL_WINDOWS):
        cols = slice(g * POOL_GROUP_DIM, (g + 1) * POOL_GROUP_DIM)
        run = pbuf_ref[:, cols]
        d = 1
        while d < w:
            run = run + pltpu.roll(run, d, 0)
            d *= 2
        tok = pbuf_ref[POOL_HALO:POOL_HALO + tm, cols]
        cnt = jnp.minimum(t_loc + 1, w).astype(F32)
        pooled = run[POOL_HALO:, :] / cnt - tok
        y = jnp.dot(pooled.astype(BF16), pm_ref[g], preferred_element_type=F32)
        ybuf_ref[:, cols] = (y * ps_ref[:, cols]).astype(BF16)

    for rows, u in zip(tiles, us):
        v_ref[rows, :] = proj(u, o3, v_w).astype(BF16)
        sgb_ref[rows, :] = jax.nn.sigmoid(proj(u, o5, d_model)).astype(BF16)
    for rows, u in zip(tiles, us):
        ya = jnp.dot(ybuf_ref[rows, :], wa_ref[...], preferred_element_type=F32)
        ma_ref[rows, :] = (jax.nn.sigmoid(proj(u, o4, d_model)) * ya).astype(BF16)


def _proj_call(x2, mix_pre_g, w_in, pool_mix, pool_scale, w_branch_a, later_weights,
               *, seq, tm, sub):
    t, d_model = x2.shape
    in_w = w_in.shape[1]
    pool_w = pool_scale.shape[1]
    v_w = pool_w
    qk_w = (in_w - pool_w - v_w - 2 * d_model) // 2
    steps = t // tm
    const = lambda i: (0, 0)
    row = lambda i: (i, 0)
    single = pl.Buffered(1)
    cast_steps, slab_in, slab_out, cast_shapes = [], [], [], []
    for w, layer in later_weights:
        _, rows_w, cols_w = w.shape
        n = max(d for d in range(1, steps + 1) if rows_w % (d * BF16_SUBLANES) == 0)
        block = (rows_w // n, cols_w)
        cast_steps.append(n)
        slab_in.append(pl.BlockSpec(
            (None,) + block, lambda i, n=n, layer=layer: (layer, jnp.minimum(i, n - 1), 0)))
        slab_out.append(pl.BlockSpec(block, lambda i, n=n: (jnp.minimum(i, n - 1), 0)))
        cast_shapes.append(jax.ShapeDtypeStruct((rows_w, cols_w), BF16))
    kern = functools.partial(_proj_kernel, cast_steps=tuple(cast_steps), tm=tm, sub=sub,
                             tiles_per_seq=seq // tm, pool_w=pool_w, qk_w=qk_w, v_w=v_w,
                             d_model=d_model)
    outs = pl.pallas_call(
        kern,
        grid=(steps,),
        in_specs=[
            pl.BlockSpec((tm, d_model), row),
            pl.BlockSpec((1, d_model), const),
            pl.BlockSpec((d_model, in_w), const, pipeline_mode=single),
            pl.BlockSpec(pool_mix.shape, lambda i: (0, 0, 0), pipeline_mode=single),
            pl.BlockSpec((1, pool_w), const),
            pl.BlockSpec((pool_w, d_model), const, pipeline_mode=single),
        ] + slab_in,
        out_specs=[
            pl.BlockSpec((tm, qk_w), row),
            pl.BlockSpec((tm, qk_w), row),
            pl.BlockSpec((tm, v_w), row),
            pl.BlockSpec((tm, d_model), row),
            pl.BlockSpec((tm, d_model), row),
        ] + slab_out,
        out_shape=[
            jax.ShapeDtypeStruct((t, qk_w), BF16),
            jax.ShapeDtypeStruct((t, qk_w), BF16),
            jax.ShapeDtypeStruct((t, v_w), BF16),
            jax.ShapeDtypeStruct((t, d_model), BF16),
            jax.ShapeDtypeStruct((t, d_model), BF16),
        ] + cast_shapes,
        scratch_shapes=[
            pltpu.VMEM((POOL_HALO + tm, pool_w), F32),
            pltpu.VMEM((tm, pool_w), BF16),
        ],
        compiler_params=pltpu.CompilerParams(
            dimension_semantics=("arbitrary",), vmem_limit_bytes=VMEM_LIMIT_BYTES),
        name="proj_pool",
    )(x2, mix_pre_g, w_in, pool_mix, pool_scale, w_branch_a, *[w for w, _ in later_weights])
    return outs[:5], outs[5:]


def _attn_kernel(slopes_ref, lq1_ref, lk1_ref, lq2_ref, lk2_ref, sg_ref,
                 q_ref, k_ref, v_ref, o_ref, kaug_ref, vaug_ref, acc1_ref, acc2_ref, qx_ref,
                 *, blk, rc, lambda_init):
    h = pl.program_id(1)
    qi = pl.program_id(2)
    slope = slopes_ref[h]
    s_len = k_ref.shape[0]

    rate = slope * LOG2E
    lane_row = lax.broadcasted_iota(jnp.int32, (1, AUG_LANES), 1)
    part = jnp.where(lane_row >= N_BIAS_LANES, lane_row - N_BIAS_LANES, lane_row) >> 1
    r0 = jnp.full((1, AUG_LANES), rate, F32)
    r_hi = r0.astype(BF16).astype(F32)
    r_mid = (r0 - r_hi).astype(BF16).astype(F32)
    r_lo = (r0 - r_hi - r_mid).astype(BF16).astype(F32)
    rate_lanes = jnp.where(part == 0, r_hi, jnp.where(part == 1, r_mid, r_lo))

    def aug_lanes(idx, lane, negate_pos):
        lo = idx & (BF16_EXACT_INT - 1)
        pos = jnp.where((lane & 1) == 0, idx - lo, lo).astype(F32)
        first = lane < N_BIAS_LANES
        second = jnp.logical_and(lane >= N_BIAS_LANES, lane < 2 * N_BIAS_LANES)
        if negate_pos:
            x = jnp.where(first, -pos, jnp.where(second, rate_lanes, 0.0))
        else:
            x = jnp.where(first, rate_lanes, jnp.where(second, pos, 0.0))
        return x.astype(BF16)

    @pl.when(qi == 0)
    def _():
        period = 2 * blk
        lane = lax.broadcasted_iota(jnp.int32, (period, AUG_LANES), 1)
        kx = aug_lanes(lax.broadcasted_iota(jnp.int32, (period, AUG_LANES), 0), lane, False)
        vx = jnp.where(lane == 0, 1.0, 0.0).astype(BF16)
        kaug_ref[:, 0:HEAD_WIDTH] = k_ref[...]
        vaug_ref[:, 0:HEAD_WIDTH] = v_ref[...]
        for r0 in range(0, s_len, period):
            kaug_ref[r0:r0 + period, HEAD_WIDTH:] = kx
            vaug_ref[r0:r0 + period, HEAD_WIDTH:] = vx
        qx_ref[...] = aug_lanes(lax.broadcasted_iota(jnp.int32, (blk, AUG_LANES), 0),
                                lax.broadcasted_iota(jnp.int32, (blk, AUG_LANES), 1), True)

    lam =(jnp.exp(jnp.sum(lq1_ref[...] * lk1_ref[...], axis=-1, keepdims=True))
           - jnp.exp(jnp.sum(lq2_ref[...] * lk2_ref[...], axis=-1, keepdims=True))
           + lambda_init)

    q = q_ref[...]
    lane = lax.broadcasted_iota(jnp.int32, q.shape, 1)
    qx = qx_ref[...]
    zero = jnp.zeros_like(q)
    q1 = jnp.concatenate([jnp.where(lane < HEAD_DIM, q, zero), qx], axis=1)
    q2 = jnp.concatenate([jnp.where(lane >= HEAD_DIM, q, zero), qx], axis=1)

    acc1_ref[...] = jnp.zeros_like(acc1_ref)
    acc2_ref[...] = jnp.zeros_like(acc2_ref)

    maps = ((q1, acc1_ref), (q2, acc2_ref))

    def scores(qr, rows, kstart, nk, mask_row0=None):
        kb = kaug_ref[pl.ds(kstart, nk), :]
        nt = (((1,), (1,)), ((), ()))
        s = lax.dot_general(qr[rows, :], kb, nt, preferred_element_type=F32)
        if mask_row0 is not None:
            causal = (lax.broadcasted_iota(jnp.int32, s.shape, 0) + mask_row0
                      >= lax.broadcasted_iota(jnp.int32, s.shape, 1))
            s = jnp.where(causal, s, MASK_VALUE)
        return s

    def update(s, ar, rows, kstart, nk, m, c):
        vb = vaug_ref[pl.ds(kstart, nk), :]
        m_new = jnp.maximum(m, jnp.max(s, axis=-1, keepdims=True) + c)
        alpha = jnp.exp2(m - m_new)
        p = jnp.exp2(s - (m_new - c))
        ar[rows, :] = alpha * ar[rows, :] + jnp.dot(p.astype(BF16), vb,
                                                    preferred_element_type=F32)
        return m_new

    every = slice(None)

    def bias_const(kj):
        return -rate * (qi * blk - (kj >> 1) * (2 * blk)).astype(F32)

    def block_pair(t, ms):
        start = pl.multiple_of(t * (2 * blk), 2 * blk)
        ss = [scores(qr, every, start, 2 * blk) for qr, _ in maps]
        return tuple(update(s, ar, every, start, 2 * blk, m, bias_const(2 * t))
                     for s, (_, ar), m in zip(ss, maps, ms))

    ms = tuple(jnp.full((blk, 1), MASK_VALUE, F32) for _ in maps)
    ms = lax.fori_loop(0, qi // 2, block_pair, ms)

    def tail(n_before):
        start = pl.multiple_of(qi * blk - n_before, blk)
        chunks = [(slice(r * rc, (r + 1) * rc), n_before + (r + 1) * rc)
                  for r in range(blk // rc)]
        ss = [[scores(qr, rows, start, nk, mask_row0=n_before + rows.start) for qr, _ in maps]
              for rows, nk in chunks]
        for (rows, nk), s_pair in zip(chunks, ss):
            for s, (_, ar), m in zip(s_pair, maps, ms):
                update(s, ar, rows, start, nk, m[rows], bias_const(qi))
            a1 = acc1_ref[rows, :]
            a2 = acc2_ref[rows, :]
            o = (a1[:, :HEAD_WIDTH] / a1[:, HEAD_WIDTH:HEAD_WIDTH + 1]
                 - lam * (a2[:, :HEAD_WIDTH] / a2[:, HEAD_WIDTH:HEAD_WIDTH + 1]))
            o_ref[rows, :] = (_rms(o, sg_ref[...]) * (1.0 - lambda_init)).astype(BF16)

    pl.when(qi % 2 == 1)(lambda: tail(blk))
    pl.when(qi % 2 == 0)(lambda: tail(0))


def _attn_call(q, k, v, slopes, lq1, lk1, lq2, lk2, subln_g, *, blk, rc, lambda_init):
    b, s, width = q.shape
    heads = width // HEAD_WIDTH
    small = lambda bi, hi, qi: (0, 0)
    kern = functools.partial(_attn_kernel, blk=blk, rc=rc, lambda_init=lambda_init)
    return pl.pallas_call(
        kern,
        grid=(b, heads, s // blk),
        in_specs=[
            pl.BlockSpec(memory_space=pltpu.SMEM),
            pl.BlockSpec((1, HEAD_DIM), small),
            pl.BlockSpec((1, HEAD_DIM), small),
            pl.BlockSpec((1, HEAD_DIM), small),
            pl.BlockSpec((1, HEAD_DIM), small),
            pl.BlockSpec((1, HEAD_WIDTH), small),
            pl.BlockSpec((None, blk, HEAD_WIDTH), lambda bi, hi, qi: (bi, qi, hi)),
            pl.BlockSpec((None, s, HEAD_WIDTH), lambda bi, hi, qi: (bi, 0, hi)),
            pl.BlockSpec((None, s, HEAD_WIDTH), lambda bi, hi, qi: (bi, 0, hi)),
        ],
        out_specs=pl.BlockSpec((None, blk, HEAD_WIDTH), lambda bi, hi, qi: (bi, qi, hi)),
        out_shape=jax.ShapeDtypeStruct((b, s, width), BF16),
        scratch_shapes=[
            pltpu.VMEM((s, HEAD_WIDTH + AUG_LANES), BF16),
            pltpu.VMEM((s, HEAD_WIDTH + AUG_LANES), BF16),
            pltpu.VMEM((blk, HEAD_WIDTH + AUG_LANES), F32),
            pltpu.VMEM((blk, HEAD_WIDTH + AUG_LANES), F32),
            pltpu.VMEM((blk, AUG_LANES), BF16),
        ],
        compiler_params=pltpu.CompilerParams(
            dimension_semantics=("parallel", "parallel", "arbitrary"),
            vmem_limit_bytes=VMEM_LIMIT_BYTES),
        name="diff_attn",
    )(slopes, lq1, lk1, lq2, lk2, subln_g, q, k, v)


def _out_ffn_kernel(x_ref, o_ref, ma_ref, sgb_ref, wb_ref, wo_ref, g_post_ref,
                    g_pre_ref, g_fpost_ref, wg_ref, wu_ref, wd_ref, out_ref, *, sub):
    tiles = [slice(r0, r0 + sub) for r0 in range(0, x_ref.shape[0], sub)]

    def mix(rows):
        yb = jnp.dot(o_ref[rows, :], wb_ref[...], preferred_element_type=F32)
        m = ma_ref[rows, :].astype(F32) + sgb_ref[rows, :].astype(F32) * yb
        return jnp.dot(m.astype(BF16), wo_ref[...], preferred_element_type=F32)

    def norms(rows, mo):
        h1 = x_ref[rows, :] + _rms(mo, g_post_ref[...])
        return h1, _rms(h1, g_pre_ref[...]).astype(BF16)

    def hidden(u):
        gate = jnp.dot(u, wg_ref[...], preferred_element_type=F32)
        up = jnp.dot(u, wu_ref[...], preferred_element_type=F32)
        return (jax.nn.silu(gate) * up).astype(BF16)

    mos = [mix(rows) for rows in tiles]
    hus = [norms(rows, mo) for rows, mo in zip(tiles, mos)]
    fs = [hidden(u) for _, u in hus]
    dns = [jnp.dot(f, wd_ref[...], preferred_element_type=F32) for f in fs]
    for rows, (h1, _), dn in zip(tiles, hus, dns):
        out_ref[rows, :] = h1 + _rms(dn, g_fpost_ref[...])


def _out_ffn_call(x2, o2, ma, sgb, w_branch_b, w_out, mix_post_g, ffn_pre_g, ffn_post_g,
                  w_gate, w_up, w_down, *, tm, sub):
    t, d_model = x2.shape
    const = lambda i: (0, 0)
    row = lambda i: (i, 0)
    single = pl.Buffered(1)

    def wspec(w):
        return pl.BlockSpec(w.shape, const, pipeline_mode=single)

    return pl.pallas_call(
        functools.partial(_out_ffn_kernel, sub=sub),
        grid=(t // tm,),
        in_specs=[
            pl.BlockSpec((tm, d_model), row),
            pl.BlockSpec((tm, o2.shape[1]), row),
            pl.BlockSpec((tm, d_model), row),
            pl.BlockSpec((tm, d_model), row),
            wspec(w_branch_b),
            wspec(w_out),
            pl.BlockSpec((1, d_model), const),
            pl.BlockSpec((1, d_model), const),
            pl.BlockSpec((1, d_model), const),
            wspec(w_gate),
            wspec(w_up),
            wspec(w_down),
        ],
        out_specs=pl.BlockSpec((tm, d_model), row),
        out_shape=jax.ShapeDtypeStruct((t, d_model), F32),
        compiler_params=pltpu.CompilerParams(
            dimension_semantics=("parallel",), vmem_limit_bytes=VMEM_LIMIT_BYTES),
        name="out_ffn",
    )(x2, o2, ma, sgb, w_branch_b, w_out, mix_post_g, ffn_pre_g, ffn_post_g,
      w_gate, w_up, w_down)


def _alibi_slopes(n):
    start = 2.0 ** (-8.0 / n)
    return jnp.asarray([start ** (i + 1) for i in range(n)], dtype=F32)


def _layer(h, layer_idx, w_in, pool_mix, pool_scale, w_branch_a, lam_q1, lam_k1, lam_q2,
           lam_k2, subln_g, w_branch_b, w_out, mix_pre_g, mix_post_g, ffn_pre_g, ffn_post_g,
           w_ffn_gate, w_ffn_up, w_ffn_down):
    b, s, d_model = h.shape
    lambda_init = 0.8 - 0.6 * math.exp(-0.3 * layer_idx)
    x2 = h.reshape(b * s, d_model)
    row = lambda a: a.reshape(1, -1)

    (q, k, v, ma, sgb), (wb16, wo16, wg16, wu16, wd16) = _proj_call(
        x2, row(mix_pre_g), w_in.astype(BF16), pool_mix.astype(BF16), row(pool_scale),
        w_branch_a.astype(BF16),
        [(w, layer_idx) for w in (w_branch_b, w_out, w_ffn_gate, w_ffn_up, w_ffn_down)],
        seq=s, tm=1024, sub=256)

    width = q.shape[1]
    heads = width // HEAD_WIDTH
    o = _attn_call(q.reshape(b, s, width), k.reshape(b, s, width), v.reshape(b, s, width),
                   _alibi_slopes(heads), row(lam_q1), row(lam_k1), row(lam_q2), row(lam_k2),
                   row(subln_g), blk=1024, rc=256, lambda_init=lambda_init)

    out = _out_ffn_call(
        x2, o.reshape(b * s, width), ma, sgb, wb16, wo16,
        row(mix_post_g), row(ffn_pre_g), row(ffn_post_g), wg16, wu16, wd16, tm=512, sub=256)
    return out.reshape(b, s, d_model)


def kernel(x, w_in, pool_mix, pool_scale, w_branch_a, lam_q1, lam_k1, lam_q2, lam_k2,
           subln_g, w_branch_b, w_out, mix_pre_g, mix_post_g, ffn_pre_g, ffn_post_g,
           w_ffn_gate, w_ffn_up, w_ffn_down):
    h = x
    for l in range(w_in.shape[0]):
        h = _layer(h, l, w_in[l], pool_mix[l], pool_scale[l], w_branch_a[l], lam_q1[l],
                   lam_k1[l], lam_q2[l], lam_k2[l], subln_g[l], w_branch_b, w_out,
                   mix_pre_g[l], mix_post_g[l], ffn_pre_g[l], ffn_post_g[l], w_ffn_gate,
                   w_ffn_up, w_ffn_down)
    return h
```

```python
import functools
import math

import jax
import jax.numpy as jnp
from jax import lax
from jax.experimental import pallas as pl
from jax.experimental.pallas import tpu as pltpu

NORM_EPS = 1e-6
POOL_WINDOWS = (2, 4, 8, 16)
POOL_GROUP_DIM = 128
POOL_HALO = 16
HEAD_DIM = 64
HEAD_WIDTH = 2 * HEAD_DIM
MASK_VALUE = -1e30
AUG_LANES = 128
BF16_EXACT_INT = 256
BF16_SUBLANES = 16
N_BIAS_LANES = 6
LOG2E = math.log2(math.e)
BF16 = jnp.bfloat16
F32 = jnp.float32

VMEM_LIMIT_BYTES = 52 * 1024 * 1024
MXU_ROWS = 256


def _tiling(seq):
    proj_tm = 4 * MXU_ROWS
    attn_blk = 4 * MXU_ROWS
    ffn_tm = 2 * MXU_ROWS
    assert seq % proj_tm == 0 and seq % (2 * attn_blk) == 0 and seq % ffn_tm == 0
    assert attn_blk >= BF16_EXACT_INT and AUG_LANES == HEAD_WIDTH
    return dict(proj_tm=proj_tm, attn_blk=attn_blk, attn_rc=MXU_ROWS, ffn_tm=ffn_tm,
                sub=MXU_ROWS)


def _rms(x, g):
    ms = jnp.mean(x * x, axis=-1, keepdims=True)
    return x * lax.rsqrt(ms + NORM_EPS) * g


def _proj_kernel(x_ref, g_ref, w_ref, pm_ref, ps_ref, wa_ref, *rest,
                 cast_steps, tm, sub, tiles_per_seq, pool_w, qk_w, v_w, d_model):
    n_cast = len(cast_steps)
    cast_in = rest[:n_cast]
    q_ref, k_ref, v_ref, ma_ref, sgb_ref = rest[n_cast:n_cast + 5]
    cast_out = rest[n_cast + 5:2 * n_cast + 5]
    pbuf_ref, ybuf_ref = rest[2 * n_cast + 5:]
    i = pl.program_id(0)
    seq_tile = i % tiles_per_seq
    for src, dst, n in zip(cast_in, cast_out, cast_steps):
        @pl.when(i < n)
        def _(src=src, dst=dst):
            dst[...] = src[...].astype(BF16)

    tiles = [slice(r0, r0 + sub) for r0 in range(0, tm, sub)]

    def proj(u, lo, width):
        return jnp.dot(u, w_ref[:, lo:lo + width], preferred_element_type=F32)

    o1 = pool_w
    o2 = o1 + qk_w
    o3 = o2 + qk_w
    o4 = o3 + v_w
    o5 = o4 + d_model
    @pl.when(seq_tile == 0)
    def _():
        pbuf_ref[0:POOL_HALO, :] = jnp.zeros((POOL_HALO, pool_w), F32)

    @pl.when(seq_tile != 0)
    def _():
        pbuf_ref[0:POOL_HALO, :] = pbuf_ref[tm:tm + POOL_HALO, :]

    us = []
    for rows in tiles:
        us.append(_rms(x_ref[rows, :], g_ref[...]).astype(BF16))
        pbuf_ref[POOL_HALO + rows.start:POOL_HALO + rows.stop, :] = proj(us[-1], 0, pool_w)
    for rows, u in zip(tiles, us):
        q_ref[rows, :] = (proj(u, o1, qk_w) * (LOG2E / math.sqrt(HEAD_DIM))).astype(BF16)
        k_ref[rows, :] = proj(u, o2, qk_w).astype(BF16)

    t_loc = seq_tile * tm + lax.broadcasted_iota(jnp.int32, (tm, 1), 0)
    for g, w in enumerate(POOL_WINDOWS):
        cols = slice(g * POOL_GROUP_DIM, (g + 1) * POOL_GROUP_DIM)
        run = pbuf_ref[:, cols]
        d = 1
        while d < w:
            run = run + pltpu.roll(run, d, 0)
            d *= 2
        tok = pbuf_ref[POOL_HALO:POOL_HALO + tm, cols]
        cnt = jnp.minimum(t_loc + 1, w).astype(F32)
        pooled = run[POOL_HALO:, :] / cnt - tok
        y = jnp.dot(pooled.astype(BF16), pm_ref[g], preferred_element_type=F32)
        ybuf_ref[:, cols] = (y * ps_ref[:, cols]).astype(BF16)

    for rows, u in zip(tiles, us):
        v_ref[rows, :] = proj(u, o3, v_w).astype(BF16)
        sgb_ref[rows, :] = jax.nn.sigmoid(proj(u, o5, d_model)).astype(BF16)
    for rows, u in zip(tiles, us):
        ya = jnp.dot(ybuf_ref[rows, :], wa_ref[...], preferred_element_type=F32)
        ma_ref[rows, :] = (jax.nn.sigmoid(proj(u, o4, d_model)) * ya).astype(BF16)


def _proj_call(x2, mix_pre_g, w_in, pool_mix, pool_scale, w_branch_a, later_weights,
               *, seq, tm, sub):
    t, d_model = x2.shape
    in_w = w_in.shape[1]
    pool_w = pool_scale.shape[1]
    v_w = pool_w
    qk_w = (in_w - pool_w - v_w - 2 * d_model) // 2
    steps = t // tm
    const = lambda i: (0, 0)
    row = lambda i: (i, 0)
    single = pl.Buffered(1)
    cast_steps, slab_in, slab_out, cast_shapes = [], [], [], []
    for w, layer in later_weights:
        _, rows_w, cols_w = w.shape
        n = max(d for d in range(1, steps + 1) if rows_w % (d * BF16_SUBLANES) == 0)
        block = (rows_w // n, cols_w)
        cast_steps.append(n)
        slab_in.append(pl.BlockSpec(
            (None,) + block, lambda i, n=n, layer=layer: (layer, jnp.minimum(i, n - 1), 0)))
        slab_out.append(pl.BlockSpec(block, lambda i, n=n: (jnp.minimum(i, n - 1), 0)))
        cast_shapes.append(jax.ShapeDtypeStruct((rows_w, cols_w), BF16))
    kern = functools.partial(_proj_kernel, cast_steps=tuple(cast_steps), tm=tm, sub=sub,
                             tiles_per_seq=seq // tm, pool_w=pool_w, qk_w=qk_w, v_w=v_w,
                             d_model=d_model)
    outs = pl.pallas_call(
        kern,
        grid=(steps,),
        in_specs=[
            pl.BlockSpec((tm, d_model), row),
            pl.BlockSpec((1, d_model), const),
            pl.BlockSpec((d_model, in_w), const, pipeline_mode=single),
            pl.BlockSpec(pool_mix.shape, lambda i: (0, 0, 0), pipeline_mode=single),
            pl.BlockSpec((1, pool_w), const),
            pl.BlockSpec((pool_w, d_model), const, pipeline_mode=single),
        ] + slab_in,
        out_specs=[
            pl.BlockSpec((tm, qk_w), row),
            pl.BlockSpec((tm, qk_w), row),
            pl.BlockSpec((tm, v_w), row),
            pl.BlockSpec((tm, d_model), row),
            pl.BlockSpec((tm, d_model), row),
        ] + slab_out,
        out_shape=[
            jax.ShapeDtypeStruct((t, qk_w), BF16),
            jax.ShapeDtypeStruct((t, qk_w), BF16),
            jax.ShapeDtypeStruct((t, v_w), BF16),
            jax.ShapeDtypeStruct((t, d_model), BF16),
            jax.ShapeDtypeStruct((t, d_model), BF16),
        ] + cast_shapes,
        scratch_shapes=[
            pltpu.VMEM((POOL_HALO + tm, pool_w), F32),
            pltpu.VMEM((tm, pool_w), BF16),
        ],
        compiler_params=pltpu.CompilerParams(
            dimension_semantics=("arbitrary",), vmem_limit_bytes=VMEM_LIMIT_BYTES),
        name="proj_pool",
    )(x2, mix_pre_g, w_in, pool_mix, pool_scale, w_branch_a, *[w for w, _ in later_weights])
    return outs[:5], outs[5:]


def _attn_kernel(slopes_ref, lq1_ref, lk1_ref, lq2_ref, lk2_ref, sg_ref,
                 q_ref, k_ref, v_ref, o_ref, kaug_ref, vaug_ref, acc1_ref, acc2_ref, qx_ref,
                 *, blk, rc, lambda_init):
    h = pl.program_id(1)
    qi = pl.program_id(2)
    slope = slopes_ref[h]
    s_len = k_ref.shape[0]

    rate = slope * LOG2E
    lane_row = lax.broadcasted_iota(jnp.int32, (1, AUG_LANES), 1)
    part = jnp.where(lane_row >= N_BIAS_LANES, lane_row - N_BIAS_LANES, lane_row) >> 1
    r0 = jnp.full((1, AUG_LANES), rate, F32)
    r_hi = r0.astype(BF16).astype(F32)
    r_mid = (r0 - r_hi).astype(BF16).astype(F32)
    r_lo = (r0 - r_hi - r_mid).astype(BF16).astype(F32)
    rate_lanes = jnp.where(part == 0, r_hi, jnp.where(part == 1, r_mid, r_lo))

    def aug_lanes(idx, lane, negate_pos):
        lo = idx & (BF16_EXACT_INT - 1)
        pos = jnp.where((lane & 1) == 0, idx - lo, lo).astype(F32)
        first = lane < N_BIAS_LANES
        second = jnp.logical_and(lane >= N_BIAS_LANES, lane < 2 * N_BIAS_LANES)
        if negate_pos:
            x = jnp.where(first, -pos, jnp.where(second, rate_lanes, 0.0))
        else:
            x = jnp.where(first, rate_lanes, jnp.where(second, pos, 0.0))
        return x.astype(BF16)

    @pl.when(qi == 0)
    def _():
        period = 2 * blk
        lane = lax.broadcasted_iota(jnp.int32, (period, AUG_LANES), 1)
        kx = aug_lanes(lax.broadcasted_iota(jnp.int32, (period, AUG_LANES), 0), lane, False)
        vx = jnp.where(lane == 0, 1.0, 0.0).astype(BF16)
        kaug_ref[:, 0:HEAD_WIDTH] = k_ref[...]
        vaug_ref[:, 0:HEAD_WIDTH] = v_ref[...]
        for r0 in range(0, s_len, period):
            kaug_ref[r0:r0 + period, HEAD_WIDTH:] = kx
            vaug_ref[r0:r0 + period, HEAD_WIDTH:] = vx
        qx_ref[...] = aug_lanes(lax.broadcasted_iota(jnp.int32, (blk, AUG_LANES), 0),
                                lax.broadcasted_iota(jnp.int32, (blk, AUG_LANES), 1), True)

    lam = (jnp.exp(jnp.sum(lq1_ref[...] * lk1_ref[...], axis=-1, keepdims=True))
           - jnp.exp(jnp.sum(lq2_ref[...] * lk2_ref[...], axis=-1, keepdims=True))
           + lambda_init)

    q = q_ref[...]
    lane = lax.broadcasted_iota(jnp.int32, q.shape, 1)
    qx = qx_ref[...]
    zero = jnp.zeros_like(q)
    q1 = jnp.concatenate([jnp.where(lane < HEAD_DIM, q, zero), qx], axis=1)
    q2 = jnp.concatenate([jnp.where(lane >= HEAD_DIM, q, zero), qx], axis=1)

    acc1_ref[...] = jnp.zeros_like(acc1_ref)
    acc2_ref[...] = jnp.zeros_like(acc2_ref)

    maps = ((q1, acc1_ref), (q2, acc2_ref))

    lower_tri = (lax.broadcasted_iota(jnp.int32, (rc, rc), 0)
                 >= lax.broadcasted_iota(jnp.int32, (rc, rc), 1))

    def scores(qr, rows, kstart, nk, diagonal=False):
        kb = kaug_ref[pl.ds(kstart, nk), :]
        nt = (((1,), (1,)), ((), ()))
        s = lax.dot_general(qr[rows, :], kb, nt, preferred_element_type=F32)
        if diagonal:
            masked = jnp.where(lower_tri, s[:, nk - rc:], MASK_VALUE)
            s = masked if nk == rc else jnp.concatenate([s[:, :nk - rc], masked], axis=1)
        return s

    def update(s, ar, rows, kstart, nk, m, c):
        vb = vaug_ref[pl.ds(kstart, nk), :]
        m_new = jnp.maximum(m, jnp.max(s, axis=-1, keepdims=True) + c)
        alpha = jnp.exp2(m - m_new)
        p = jnp.exp2(s - (m_new - c))
        ar[rows, :] = alpha * ar[rows, :] + jnp.dot(p.astype(BF16), vb,
                                                    preferred_element_type=F32)
        return m_new

    every = slice(None)

    def bias_const(kj):
        return -rate * (qi * blk - (kj >> 1) * (2 * blk)).astype(F32)

    def block_pair(t, ms):
        start = pl.multiple_of(t * (2 * blk), 2 * blk)
        ss = [scores(qr, every, start, 2 * blk) for qr, _ in maps]
        return tuple(update(s, ar, every, start, 2 * blk, m, bias_const(2 * t))
                     for s, (_, ar), m in zip(ss, maps, ms))

    ms = tuple(jnp.full((blk, 1), MASK_VALUE, F32) for _ in maps)
    ms = lax.fori_loop(0, qi // 2, block_pair, ms)

    def tail(n_before):
        start = pl.multiple_of(qi * blk - n_before, blk)
        chunks = [(slice(r * rc, (r + 1) * rc), n_before + (r + 1) * rc)
                  for r in range(blk // rc)]
        ss = [[scores(qr, rows, start, nk, diagonal=True) for qr, _ in maps]
              for rows, nk in chunks]
        for (rows, nk), s_pair in zip(chunks, ss):
            for s, (_, ar), m in zip(s_pair, maps, ms):
                update(s, ar, rows, start, nk, m[rows], bias_const(qi))
            a1 = acc1_ref[rows, :]
            a2 = acc2_ref[rows, :]
            o = (a1[:, :HEAD_WIDTH] / a1[:, HEAD_WIDTH:HEAD_WIDTH + 1]
                 - lam * (a2[:, :HEAD_WIDTH] / a2[:, HEAD_WIDTH:HEAD_WIDTH + 1]))
            o_ref[rows, :] = (_rms(o, sg_ref[...]) * (1.0 - lambda_init)).astype(BF16)

    pl.when(qi % 2 == 1)(lambda: tail(blk))
    pl.when(qi % 2 == 0)(lambda: tail(0))


def _attn_call(q, k, v, slopes, lq1, lk1, lq2, lk2, subln_g, *, blk, rc, lambda_init):
    b, s, width = q.shape
    heads = width // HEAD_WIDTH
    small = lambda bi, hi, qi: (0, 0)
    kern = functools.partial(_attn_kernel, blk=blk, rc=rc, lambda_init=lambda_init)
    return pl.pallas_call(
        kern,
        grid=(b, heads, s // blk),
        in_specs=[
            pl.BlockSpec(memory_space=pltpu.SMEM),
            pl.BlockSpec((1, HEAD_DIM), small),
            pl.BlockSpec((1, HEAD_DIM), small),
            pl.BlockSpec((1, HEAD_DIM), small),
            pl.BlockSpec((1, HEAD_DIM), small),
            pl.BlockSpec((1, HEAD_WIDTH), small),
            pl.BlockSpec((None, blk, HEAD_WIDTH), lambda bi, hi, qi: (bi, qi, hi)),
            pl.BlockSpec((None, s, HEAD_WIDTH), lambda bi, hi, qi: (bi, 0, hi)),
            pl.BlockSpec((None, s, HEAD_WIDTH), lambda bi, hi, qi: (bi, 0, hi)),
        ],
        out_specs=pl.BlockSpec((None, blk, HEAD_WIDTH), lambda bi, hi, qi: (bi, qi, hi)),
        out_shape=jax.ShapeDtypeStruct((b, s, width), BF16),
        scratch_shapes=[
            pltpu.VMEM((s, HEAD_WIDTH + AUG_LANES), BF16),
            pltpu.VMEM((s, HEAD_WIDTH + AUG_LANES), BF16),
            pltpu.VMEM((blk, HEAD_WIDTH + AUG_LANES), F32),
            pltpu.VMEM((blk, HEAD_WIDTH + AUG_LANES), F32),
            pltpu.VMEM((blk, AUG_LANES), BF16),
        ],
        compiler_params=pltpu.CompilerParams(
            dimension_semantics=("parallel", "parallel", "arbitrary"),
            vmem_limit_bytes=VMEM_LIMIT_BYTES),
        name="diff_attn",
    )(slopes, lq1, lk1, lq2, lk2, subln_g, q, k, v)


def _out_ffn_kernel(x_ref, o_ref, ma_ref, sgb_ref, wb_ref, wo_ref, g_post_ref,
                    g_pre_ref, g_fpost_ref, wg_ref, wu_ref, wd_ref, out_ref, *, sub):
    tiles = [slice(r0, r0 + sub) for r0 in range(0, x_ref.shape[0], sub)]

    def mix(rows):
        yb = jnp.dot(o_ref[rows, :], wb_ref[...], preferred_element_type=F32)
        m = ma_ref[rows, :].astype(F32) + sgb_ref[rows, :].astype(F32) * yb
        return jnp.dot(m.astype(BF16), wo_ref[...], preferred_element_type=F32)

    def norms(rows, mo):
        h1 = x_ref[rows, :] + _rms(mo, g_post_ref[...])
        return h1, _rms(h1, g_pre_ref[...]).astype(BF16)

    def hidden(u):
        gate = jnp.dot(u, wg_ref[...], preferred_element_type=F32)
        up = jnp.dot(u, wu_ref[...], preferred_element_type=F32)
        return (jax.nn.silu(gate) * up).astype(BF16)

    mos = [mix(rows) for rows in tiles]
    hus = [norms(rows, mo) for rows, mo in zip(tiles, mos)]
    fs = [hidden(u) for _, u in hus]
    dns = [jnp.dot(f, wd_ref[...], preferred_element_type=F32) for f in fs]
    for rows, (h1, _), dn in zip(tiles, hus, dns):
        out_ref[rows, :] = h1 + _rms(dn, g_fpost_ref[...])


def _out_ffn_call(x2, o2, ma, sgb, w_branch_b, w_out, mix_post_g, ffn_pre_g, ffn_post_g,
                  w_gate, w_up, w_down, *, tm, sub):
    t, d_model = x2.shape
    const = lambda i: (0, 0)
    row = lambda i: (i, 0)
    single = pl.Buffered(1)

    def wspec(w):
        return pl.BlockSpec(w.shape, const, pipeline_mode=single)

    return pl.pallas_call(
        functools.partial(_out_ffn_kernel, sub=sub),
        grid=(t // tm,),
        in_specs=[
            pl.BlockSpec((tm, d_model), row),
            pl.BlockSpec((tm, o2.shape[1]), row),
            pl.BlockSpec((tm, d_model), row),
            pl.BlockSpec((tm, d_model), row),
            wspec(w_branch_b),
            wspec(w_out),
            pl.BlockSpec((1, d_model), const),
            pl.BlockSpec((1, d_model), const),
            pl.BlockSpec((1, d_model), const),
            wspec(w_gate),
            wspec(w_up),
            wspec(w_down),
        ],
        out_specs=pl.BlockSpec((tm, d_model), row),
        out_shape=jax.ShapeDtypeStruct((t, d_model), F32),
        compiler_params=pltpu.CompilerParams(
            dimension_semantics=("parallel",), vmem_limit_bytes=VMEM_LIMIT_BYTES),
        name="out_ffn",
    )(x2, o2, ma, sgb, w_branch_b, w_out, mix_post_g, ffn_pre_g, ffn_post_g,
      w_gate, w_up, w_down)


def _alibi_slopes(n):
    start = 2.0 ** (-8.0 / n)
    return jnp.asarray([start ** (i + 1) for i in range(n)], dtype=F32)


def _layer(h, layer_idx, w_in, pool_mix, pool_scale, w_branch_a, lam_q1, lam_k1, lam_q2,
           lam_k2, subln_g, w_branch_b, w_out, mix_pre_g, mix_post_g, ffn_pre_g, ffn_post_g,
           w_ffn_gate, w_ffn_up, w_ffn_down):
    b, s, d_model = h.shape
    lambda_init = 0.8 - 0.6 * math.exp(-0.3 * layer_idx)
    x2 = h.reshape(b * s, d_model)
    tiles = _tiling(s)
    row = lambda a: a.reshape(1, -1)

    (q, k, v, ma, sgb), (wb16, wo16, wg16, wu16, wd16) = _proj_call(
        x2, row(mix_pre_g), w_in.astype(BF16), pool_mix.astype(BF16), row(pool_scale),
        w_branch_a.astype(BF16),
        [(w, layer_idx) for w in (w_branch_b, w_out, w_ffn_gate, w_ffn_up, w_ffn_down)],
        seq=s, tm=tiles["proj_tm"], sub=tiles["sub"])

    width = q.shape[1]
    heads = width // HEAD_WIDTH
    o = _attn_call(q.reshape(b, s, width), k.reshape(b, s, width), v.reshape(b, s, width),
                   _alibi_slopes(heads), row(lam_q1), row(lam_k1), row(lam_q2), row(lam_k2),
                   row(subln_g), blk=tiles["attn_blk"], rc=tiles["attn_rc"],
                   lambda_init=lambda_init)

    out = _out_ffn_call(
        x2, o.reshape(b * s, width), ma, sgb, wb16, wo16,
        row(mix_post_g), row(ffn_pre_g), row(ffn_post_g), wg16, wu16, wd16,
        tm=tiles["ffn_tm"], sub=tiles["sub"])
    return out.reshape(b, s, d_model)


def kernel(x, w_in, pool_mix, pool_scale, w_branch_a, lam_q1, lam_k1, lam_q2, lam_k2,
           subln_g, w_branch_b, w_out, mix_pre_g, mix_post_g, ffn_pre_g, ffn_post_g,
           w_ffn_gate, w_ffn_up, w_ffn_down):
    h = x
    for l in range(w_in.shape[0]):
        h = _layer(h, l, w_in[l], pool_mix[l], pool_scale[l], w_branch_a[l], lam_q1[l],
                   lam_k1[l], lam_q2[l], lam_k2[l], subln_g[l], w_branch_b, w_out,
                   mix_pre_g[l], mix_post_g[l], ffn_pre_g[l], ffn_post_g[l], w_ffn_gate,
                   w_ffn_up, w_ffn_down)
    return h
```

```python
import functools
import math

import jax
import jax.numpy as jnp
from jax import lax
from jax.experimental import pallas as pl
from jax.experimental.pallas import tpu as pltpu

NORM_EPS = 1e-6
POOL_WINDOWS = (2, 4, 8, 16)
POOL_GROUP_DIM = 128
POOL_HALO = 16
HEAD_DIM = 64
HEAD_WIDTH = 2 * HEAD_DIM
MASK_VALUE = -1e30
AUG_LANES = 128
BF16_EXACT_INT = 256
BF16_SUBLANES = 16
N_BIAS_LANES = 6
LOG2E = math.log2(math.e)
BF16 = jnp.bfloat16
F32 = jnp.float32

VMEM_LIMIT_BYTES = 52 * 1024 * 1024
FFN_VMEM_LIMIT_BYTES = 58 * 1024 * 1024
MXU_ROWS = 256


def _tiling(seq):
    proj_tm = 4 * MXU_ROWS
    attn_blk = 4 * MXU_ROWS
    ffn_tm = 4 * MXU_ROWS
    assert seq % proj_tm == 0 and seq % (2 * attn_blk) == 0 and seq % ffn_tm == 0
    assert attn_blk >= BF16_EXACT_INT and AUG_LANES == HEAD_WIDTH
    return dict(proj_tm=proj_tm, attn_blk=attn_blk, attn_rc=MXU_ROWS, ffn_tm=ffn_tm,
                sub=MXU_ROWS)


def _rms(x, g):
    ms = jnp.mean(x * x, axis=-1, keepdims=True)
    return x * lax.rsqrt(ms + NORM_EPS) * g


def _proj_kernel(x_ref, g_ref, w_ref, pm_ref, ps_ref, wa_ref, *rest,
                 cast_steps, tm, sub, tiles_per_seq, pool_w, qk_w, v_w, d_model):
    n_cast = len(cast_steps)
    cast_in = rest[:n_cast]
    q_ref, k_ref, v_ref, ma_ref, sgb_ref = rest[n_cast:n_cast + 5]
    cast_out = rest[n_cast + 5:2 * n_cast + 5]
    pbuf_ref, ybuf_ref = rest[2 * n_cast + 5:]
    i = pl.program_id(0)
    seq_tile = i % tiles_per_seq
    for src, dst, n in zip(cast_in, cast_out, cast_steps):
        @pl.when(i < n)
        def _(src=src, dst=dst):
            dst[...] = src[...].astype(BF16)

    tiles = [slice(r0, r0 + sub) for r0 in range(0, tm, sub)]

    def proj(u, lo, width):
        return jnp.dot(u, w_ref[:, lo:lo + width], preferred_element_type=F32)

    o1 = pool_w
    o2 = o1 + qk_w
    o3 = o2 + qk_w
    o4 = o3 + v_w
    o5 = o4 + d_model
    @pl.when(seq_tile == 0)
    def _():
        pbuf_ref[0:POOL_HALO, :] = jnp.zeros((POOL_HALO, pool_w), F32)

    @pl.when(seq_tile != 0)
    def _():
        pbuf_ref[0:POOL_HALO, :] = pbuf_ref[tm:tm + POOL_HALO, :]

    us = []
    for rows in tiles:
        us.append(_rms(x_ref[rows, :], g_ref[...]).astype(BF16))
        pbuf_ref[POOL_HALO + rows.start:POOL_HALO + rows.stop, :] = proj(us[-1], 0, pool_w)
    for rows, u in zip(tiles, us):
        q_ref[rows, :] = (proj(u, o1, qk_w) * (LOG2E / math.sqrt(HEAD_DIM))).astype(BF16)
        k_ref[rows, :] = proj(u, o2, qk_w).astype(BF16)

    t_loc = seq_tile * tm + lax.broadcasted_iota(jnp.int32, (tm, 1), 0)
    for g, w in enumerate(POOL_WINDOWS):
        cols = slice(g * POOL_GROUP_DIM, (g + 1) * POOL_GROUP_DIM)
        run = pbuf_ref[:, cols]
        d = 1
        while d < w:
            run = run + pltpu.roll(run, d, 0)
            d *= 2
        tok = pbuf_ref[POOL_HALO:POOL_HALO + tm, cols]
        cnt = jnp.minimum(t_loc + 1, w).astype(F32)
        pooled = run[POOL_HALO:, :] / cnt - tok
        y = jnp.dot(pooled.astype(BF16), pm_ref[g], preferred_element_type=F32)
        ybuf_ref[:, cols] = (y * ps_ref[:, cols]).astype(BF16)

    for rows, u in zip(tiles, us):
        v_ref[rows, :] = proj(u, o3, v_w).astype(BF16)
        sgb_ref[rows, :] = jax.nn.sigmoid(proj(u, o5, d_model)).astype(BF16)
    for rows, u in zip(tiles, us):
        ya = jnp.dot(ybuf_ref[rows, :], wa_ref[...], preferred_element_type=F32)
        ma_ref[rows, :] = (jax.nn.sigmoid(proj(u, o4, d_model)) * ya).astype(BF16)


def _proj_call(x2, mix_pre_g, w_in, pool_mix, pool_scale, w_branch_a, later_weights,
               *, seq, tm, sub):
    t, d_model = x2.shape
    in_w = w_in.shape[1]
    pool_w = pool_scale.shape[1]
    v_w = pool_w
    qk_w = (in_w - pool_w - v_w - 2 * d_model) // 2
    steps = t // tm
    const = lambda i: (0, 0)
    row = lambda i: (i, 0)
    single = pl.Buffered(1)
    cast_steps, slab_in, slab_out, cast_shapes = [], [], [], []
    for w, layer in later_weights:
        _, rows_w, cols_w = w.shape
        n = max(d for d in range(1, steps + 1) if rows_w % (d * BF16_SUBLANES) == 0)
        block = (rows_w // n, cols_w)
        cast_steps.append(n)
        slab_in.append(pl.BlockSpec(
            (None,) + block, lambda i, n=n, layer=layer: (layer, jnp.minimum(i, n - 1), 0)))
        slab_out.append(pl.BlockSpec(block, lambda i, n=n: (jnp.minimum(i, n - 1), 0)))
        cast_shapes.append(jax.ShapeDtypeStruct((rows_w, cols_w), BF16))
    kern = functools.partial(_proj_kernel, cast_steps=tuple(cast_steps), tm=tm, sub=sub,
                             tiles_per_seq=seq // tm, pool_w=pool_w, qk_w=qk_w, v_w=v_w,
                             d_model=d_model)
    outs = pl.pallas_call(
        kern,
        grid=(steps,),
        in_specs=[
            pl.BlockSpec((tm, d_model), row),
            pl.BlockSpec((1, d_model), const),
            pl.BlockSpec((d_model, in_w), const, pipeline_mode=single),
            pl.BlockSpec(pool_mix.shape, lambda i: (0, 0, 0), pipeline_mode=single),
            pl.BlockSpec((1, pool_w), const),
            pl.BlockSpec((pool_w, d_model), const, pipeline_mode=single),
        ] + slab_in,
        out_specs=[
            pl.BlockSpec((tm, qk_w), row),
            pl.BlockSpec((tm, qk_w), row),
            pl.BlockSpec((tm, v_w), row),
            pl.BlockSpec((tm, d_model), row),
            pl.BlockSpec((tm, d_model), row),
        ] + slab_out,
        out_shape=[
            jax.ShapeDtypeStruct((t, qk_w), BF16),
            jax.ShapeDtypeStruct((t, qk_w), BF16),
            jax.ShapeDtypeStruct((t, v_w), BF16),
            jax.ShapeDtypeStruct((t, d_model), BF16),
            jax.ShapeDtypeStruct((t, d_model), BF16),
        ] + cast_shapes,
        scratch_shapes=[
            pltpu.VMEM((POOL_HALO + tm, pool_w), F32),
            pltpu.VMEM((tm, pool_w), BF16),
        ],
        compiler_params=pltpu.CompilerParams(
            dimension_semantics=("arbitrary",), vmem_limit_bytes=VMEM_LIMIT_BYTES),
        name="proj_pool",
    )(x2, mix_pre_g, w_in, pool_mix, pool_scale, w_branch_a, *[w for w, _ in later_weights])
    return outs[:5], outs[5:]


def _attn_kernel(slopes_ref, lq1_ref, lk1_ref, lq2_ref, lk2_ref, sg_ref,
                 q_ref, k_ref, v_ref, o_ref, kaug_ref, vaug_ref, acc1_ref, acc2_ref, qx_ref,
                 *, blk, rc, lambda_init):
    h = pl.program_id(1)
    qi = pl.program_id(2)
    slope = slopes_ref[h]
    s_len = k_ref.shape[0]

    rate = slope * LOG2E
    lane_row = lax.broadcasted_iota(jnp.int32, (1, AUG_LANES), 1)
    part = jnp.where(lane_row >= N_BIAS_LANES, lane_row - N_BIAS_LANES, lane_row) >> 1
    r0 = jnp.full((1, AUG_LANES), rate, F32)
    r_hi = r0.astype(BF16).astype(F32)
    r_mid = (r0 - r_hi).astype(BF16).astype(F32)
    r_lo = (r0 - r_hi - r_mid).astype(BF16).astype(F32)
    rate_lanes = jnp.where(part == 0, r_hi, jnp.where(part == 1, r_mid, r_lo))

    def aug_lanes(idx, lane, negate_pos):
        lo = idx & (BF16_EXACT_INT - 1)
        pos = jnp.where((lane & 1) == 0, idx - lo, lo).astype(F32)
        first = lane < N_BIAS_LANES
        second = jnp.logical_and(lane >= N_BIAS_LANES, lane < 2 * N_BIAS_LANES)
        if negate_pos:
            x = jnp.where(first, -pos, jnp.where(second, rate_lanes, 0.0))
        else:
            x = jnp.where(first, rate_lanes, jnp.where(second, pos, 0.0))
        return x.astype(BF16)

    @pl.when(qi == 0)
    def _():
        period = 2 * blk
        lane = lax.broadcasted_iota(jnp.int32, (period, AUG_LANES), 1)
        kx = aug_lanes(lax.broadcasted_iota(jnp.int32, (period, AUG_LANES), 0), lane, False)
        vx = jnp.where(lane == 0, 1.0, 0.0).astype(BF16)
        kaug_ref[:, 0:HEAD_WIDTH] = k_ref[...]
        vaug_ref[:, 0:HEAD_WIDTH] = v_ref[...]
        for r0 in range(0, s_len, period):
            kaug_ref[r0:r0 + period, HEAD_WIDTH:] = kx
            vaug_ref[r0:r0 + period, HEAD_WIDTH:] = vx
        qx_ref[...] = aug_lanes(lax.broadcasted_iota(jnp.int32, (blk, AUG_LANES), 0),
                                lax.broadcasted_iota(jnp.int32, (blk, AUG_LANES), 1), True)

    lam = (jnp.exp(jnp.sum(lq1_ref[...] * lk1_ref[...], axis=-1, keepdims=True))
           - jnp.exp(jnp.sum(lq2_ref[...] * lk2_ref[...], axis=-1, keepdims=True))
           + lambda_init)

    q = q_ref[...]
    lane = lax.broadcasted_iota(jnp.int32, q.shape, 1)
    qx = qx_ref[...]
    zero = jnp.zeros_like(q)
    q1 = jnp.concatenate([jnp.where(lane < HEAD_DIM, q, zero), qx], axis=1)
    q2 = jnp.concatenate([jnp.where(lane >= HEAD_DIM, q, zero), qx], axis=1)

    acc1_ref[...] = jnp.zeros_like(acc1_ref)
    acc2_ref[...] = jnp.zeros_like(acc2_ref)

    maps = ((q1, acc1_ref), (q2, acc2_ref))

    lower_tri = (lax.broadcasted_iota(jnp.int32, (rc, rc), 0)
                 >= lax.broadcasted_iota(jnp.int32, (rc, rc), 1))

    def scores(qr, rows, kstart, nk, diagonal=False):
        kb = kaug_ref[pl.ds(kstart, nk), :]
        nt = (((1,), (1,)), ((), ()))
        s = lax.dot_general(qr[rows, :], kb, nt, preferred_element_type=F32)
        if diagonal:
            masked = jnp.where(lower_tri, s[:, nk - rc:], MASK_VALUE)
            s = masked if nk == rc else jnp.concatenate([s[:, :nk - rc], masked], axis=1)
        return s

    def update(s, ar, rows, kstart, nk, m, c):
        vb = vaug_ref[pl.ds(kstart, nk), :]
        m_new = jnp.maximum(m, jnp.max(s, axis=-1, keepdims=True) + c)
        alpha = jnp.exp2(m - m_new)
        p = jnp.exp2(s - (m_new - c))
        ar[rows, :] = alpha * ar[rows, :] + jnp.dot(p.astype(BF16), vb,
                                                    preferred_element_type=F32)
        return m_new

    every = slice(None)

    def bias_const(kj):
        return -rate * (qi * blk - (kj >> 1) * (2 * blk)).astype(F32)

    def block_pair(t, ms):
        start = pl.multiple_of(t * (2 * blk), 2 * blk)
        ss = [scores(qr, every, start, 2 * blk) for qr, _ in maps]
        return tuple(update(s, ar, every, start, 2 * blk, m, bias_const(2 * t))
                     for s, (_, ar), m in zip(ss, maps, ms))

    ms = tuple(jnp.full((blk, 1), MASK_VALUE, F32) for _ in maps)
    ms = lax.fori_loop(0, qi // 2, block_pair, ms)

    def tail(n_before):
        start = pl.multiple_of(qi * blk - n_before, blk)
        chunks = [(slice(r * rc, (r + 1) * rc), n_before + (r + 1) * rc)
                  for r in range(blk // rc)]
        ss = [[scores(qr, rows, start, nk, diagonal=True) for qr, _ in maps]
              for rows, nk in chunks]
        for (rows, nk), s_pair in zip(chunks, ss):
            for s, (_, ar), m in zip(s_pair, maps, ms):
                update(s, ar, rows, start, nk, m[rows], bias_const(qi))
            a1 = acc1_ref[rows, :]
            a2 = acc2_ref[rows, :]
            o = (a1[:, :HEAD_WIDTH] / a1[:, HEAD_WIDTH:HEAD_WIDTH + 1]
                 - lam * (a2[:, :HEAD_WIDTH] / a2[:, HEAD_WIDTH:HEAD_WIDTH + 1]))
            o_ref[rows, :] = (_rms(o, sg_ref[...]) * (1.0 - lambda_init)).astype(BF16)

    pl.when(qi % 2 == 1)(lambda: tail(blk))
    pl.when(qi % 2 == 0)(lambda: tail(0))


def _attn_call(q, k, v, slopes, lq1, lk1, lq2, lk2, subln_g, *, blk, rc, lambda_init):
    b, s, width = q.shape
    heads = width // HEAD_WIDTH
    small = lambda bi, hi, qi: (0, 0)
    kern = functools.partial(_attn_kernel, blk=blk, rc=rc, lambda_init=lambda_init)
    return pl.pallas_call(
        kern,
        grid=(b, heads, s // blk),
        in_specs=[
            pl.BlockSpec(memory_space=pltpu.SMEM),
            pl.BlockSpec((1, HEAD_DIM), small),
            pl.BlockSpec((1, HEAD_DIM), small),
            pl.BlockSpec((1, HEAD_DIM), small),
            pl.BlockSpec((1, HEAD_DIM), small),
            pl.BlockSpec((1, HEAD_WIDTH), small),
            pl.BlockSpec((None, blk, HEAD_WIDTH), lambda bi, hi, qi: (bi, qi, hi)),
            pl.BlockSpec((None, s, HEAD_WIDTH), lambda bi, hi, qi: (bi, 0, hi)),
            pl.BlockSpec((None, s, HEAD_WIDTH), lambda bi, hi, qi: (bi, 0, hi)),
        ],
        out_specs=pl.BlockSpec((None, blk, HEAD_WIDTH), lambda bi, hi, qi: (bi, qi, hi)),
        out_shape=jax.ShapeDtypeStruct((b, s, width), BF16),
        scratch_shapes=[
            pltpu.VMEM((s, HEAD_WIDTH + AUG_LANES), BF16),
            pltpu.VMEM((s, HEAD_WIDTH + AUG_LANES), BF16),
            pltpu.VMEM((blk, HEAD_WIDTH + AUG_LANES), F32),
            pltpu.VMEM((blk, HEAD_WIDTH + AUG_LANES), F32),
            pltpu.VMEM((blk, AUG_LANES), BF16),
        ],
        compiler_params=pltpu.CompilerParams(
            dimension_semantics=("parallel", "parallel", "arbitrary"),
            vmem_limit_bytes=VMEM_LIMIT_BYTES),
        name="diff_attn",
    )(slopes, lq1, lk1, lq2, lk2, subln_g, q, k, v)


def _out_ffn_kernel(x_ref, o_ref, ma_ref, sgb_ref, wb_ref, wo_ref, g_post_ref,
                    g_pre_ref, g_fpost_ref, wg_ref, wu_ref, wd_ref, out_ref, *, sub):
    tiles = [slice(r0, r0 + sub) for r0 in range(0, x_ref.shape[0], sub)]

    def branch_b(rows):
        return jnp.dot(o_ref[rows, :], wb_ref[...], preferred_element_type=F32)

    def mix(rows, yb):
        m = ma_ref[rows, :].astype(F32) + sgb_ref[rows, :].astype(F32) * yb
        return jnp.dot(m.astype(BF16), wo_ref[...], preferred_element_type=F32)

    def norms(rows, mo):
        h1 = x_ref[rows, :] + _rms(mo, g_post_ref[...])
        return h1, _rms(h1, g_pre_ref[...]).astype(BF16)

    def hidden(u):
        gate = jnp.dot(u, wg_ref[...], preferred_element_type=F32)
        up = jnp.dot(u, wu_ref[...], preferred_element_type=F32)
        return (jax.nn.silu(gate) * up).astype(BF16)

    for g0 in range(0, len(tiles), 2):
        group = tiles[g0:g0 + 2]
        ybs = [branch_b(rows) for rows in group]
        mos = [mix(rows, yb) for rows, yb in zip(group, ybs)]
        hus = [norms(rows, mo) for rows, mo in zip(group, mos)]
        fs = [hidden(u) for _, u in hus]
        dns = [jnp.dot(f, wd_ref[...], preferred_element_type=F32) for f in fs]
        for rows, (h1, _), dn in zip(group, hus, dns):
            out_ref[rows, :] = h1 + _rms(dn, g_fpost_ref[...])


def _out_ffn_call(x2, o2, ma, sgb, w_branch_b, w_out, mix_post_g, ffn_pre_g, ffn_post_g,
                  w_gate, w_up, w_down, *, tm, sub):
    t, d_model = x2.shape
    const = lambda i: (0, 0)
    row = lambda i: (i, 0)
    single = pl.Buffered(1)

    def wspec(w):
        return pl.BlockSpec(w.shape, const, pipeline_mode=single)

    return pl.pallas_call(
        functools.partial(_out_ffn_kernel, sub=sub),
        grid=(t // tm,),
        in_specs=[
            pl.BlockSpec((tm, d_model), row),
            pl.BlockSpec((tm, o2.shape[1]), row),
            pl.BlockSpec((tm, d_model), row),
            pl.BlockSpec((tm, d_model), row),
            wspec(w_branch_b),
            wspec(w_out),
            pl.BlockSpec((1, d_model), const),
            pl.BlockSpec((1, d_model), const),
            pl.BlockSpec((1, d_model), const),
            wspec(w_gate),
            wspec(w_up),
            wspec(w_down),
        ],
        out_specs=pl.BlockSpec((tm, d_model), row),
        out_shape=jax.ShapeDtypeStruct((t, d_model), F32),
        compiler_params=pltpu.CompilerParams(
            dimension_semantics=("parallel",), vmem_limit_bytes=FFN_VMEM_LIMIT_BYTES),
        name="out_ffn",
    )(x2, o2, ma, sgb, w_branch_b, w_out, mix_post_g, ffn_pre_g, ffn_post_g,
      w_gate, w_up, w_down)


def _alibi_slopes(n):
    start = 2.0 ** (-8.0 / n)
    return jnp.asarray([start ** (i + 1) for i in range(n)], dtype=F32)


def _layer(h, layer_idx, w_in, pool_mix, pool_scale, w_branch_a, lam_q1, lam_k1, lam_q2,
           lam_k2, subln_g, w_branch_b, w_out, mix_pre_g, mix_post_g, ffn_pre_g, ffn_post_g,
           w_ffn_gate, w_ffn_up, w_ffn_down):
    b, s, d_model = h.shape
    lambda_init = 0.8 - 0.6 * math.exp(-0.3 * layer_idx)
    x2 = h.reshape(b * s, d_model)
    tiles = _tiling(s)
    row = lambda a: a.reshape(1, -1)

    (q, k, v, ma, sgb), (wb16, wo16, wg16, wu16, wd16) = _proj_call(
        x2, row(mix_pre_g), w_in.astype(BF16), pool_mix.astype(BF16), row(pool_scale),
        w_branch_a.astype(BF16),
        [(w, layer_idx) for w in (w_branch_b, w_out, w_ffn_gate, w_ffn_up, w_ffn_down)],
        seq=s, tm=tiles["proj_tm"], sub=tiles["sub"])

    width = q.shape[1]
    heads = width // HEAD_WIDTH
    o = _attn_call(q.reshape(b, s, width), k.reshape(b, s, width), v.reshape(b, s, width),
                   _alibi_slopes(heads), row(lam_q1), row(lam_k1), row(lam_q2), row(lam_k2),
                   row(subln_g), blk=tiles["attn_blk"], rc=tiles["attn_rc"],
                   lambda_init=lambda_init)

    out = _out_ffn_call(
        x2, o.reshape(b * s, width), ma, sgb, wb16, wo16,
        row(mix_post_g), row(ffn_pre_g), row(ffn_post_g), wg16, wu16, wd16,
        tm=tiles["ffn_tm"], sub=tiles["sub"])
    return out.reshape(b, s, d_model)


def kernel(x, w_in, pool_mix, pool_scale, w_branch_a, lam_q1, lam_k1, lam_q2, lam_k2,
           subln_g, w_branch_b, w_out, mix_pre_g, mix_post_g, ffn_pre_g, ffn_post_g,
           w_ffn_gate, w_ffn_up, w_ffn_down):
    h = x
    for l in range(w_in.shape[0]):
        h = _layer(h, l, w_in[l], pool_mix[l], pool_scale[l], w_branch_a[l], lam_q1[l],
                   lam_k1[l], lam_q2[l], lam_k2[l], subln_g[l], w_branch_b, w_out,
                   mix_pre_g[l], mix_post_g[l], ffn_pre_g[l], ffn_post_g[l], w_ffn_gate,
                   w_ffn_up, w_ffn_down)
    return h
```

```python
import functools
import math

import jax
import jax.numpy as jnp
from jax import lax
from jax.experimental import pallas as pl
from jax.experimental.pallas import tpu as pltpu

NORM_EPS = 1e-6
POOL_WINDOWS = (2, 4, 8, 16)
POOL_GROUP_DIM = 128
POOL_HALO = 16
HEAD_DIM = 64
HEAD_WIDTH = 2 * HEAD_DIM
MASK_VALUE = -1e30
AUG_LANES = 128
BF16_EXACT_INT = 256
BF16_SUBLANES = 16
N_BIAS_LANES = 6
LOG2E = math.log2(math.e)
BF16 = jnp.bfloat16
F32 = jnp.float32

VMEM_LIMIT_BYTES = 52 * 1024 * 1024
MXU_ROWS = 256


def _tiling(seq):
    proj_tm = 4 * MXU_ROWS
    attn_blk = 4 * MXU_ROWS
    ffn_tm = 2 * MXU_ROWS
    assert seq % proj_tm == 0 and seq % (2 * attn_blk) == 0 and seq % ffn_tm == 0
    assert attn_blk >= BF16_EXACT_INT and AUG_LANES == HEAD_WIDTH
    return dict(proj_tm=proj_tm, attn_blk=attn_blk, attn_rc=MXU_ROWS, ffn_tm=ffn_tm,
                sub=MXU_ROWS)


def _rms(x, g):
    ms = jnp.mean(x * x, axis=-1, keepdims=True)
    return x * lax.rsqrt(ms + NORM_EPS) * g


def _proj_kernel(x_ref, g_ref, w_ref, pm_ref, ps_ref, wa_ref, *rest,
                 cast_steps, tm, sub, tiles_per_seq, pool_w, qk_w, v_w, d_model):
    n_cast = len(cast_steps)
    cast_in = rest[:n_cast]
    q_ref, k_ref, v_ref, ma_ref, sgb_ref = rest[n_cast:n_cast + 5]
    cast_out = rest[n_cast + 5:2 * n_cast + 5]
    pbuf_ref, ybuf_ref = rest[2 * n_cast + 5:]
    i = pl.program_id(0)
    seq_tile = i % tiles_per_seq
    for src, dst, n in zip(cast_in, cast_out, cast_steps):
        @pl.when(i < n)
        def _(src=src, dst=dst):
            dst[...] = src[...].astype(BF16)

    tiles = [slice(r0, r0 + sub) for r0 in range(0, tm, sub)]

    def proj(u, lo, width):
        return jnp.dot(u, w_ref[:, lo:lo + width], preferred_element_type=F32)

    o1 = pool_w
    o2 = o1 + qk_w
    o3 = o2 + qk_w
    o4 = o3 + v_w
    o5 = o4 + d_model
    @pl.when(seq_tile == 0)
    def _():
        pbuf_ref[0:POOL_HALO, :] = jnp.zeros((POOL_HALO, pool_w), F32)

    @pl.when(seq_tile != 0)
    def _():
        pbuf_ref[0:POOL_HALO, :] = pbuf_ref[tm:tm + POOL_HALO, :]

    us = []
    for rows in tiles:
        us.append(_rms(x_ref[rows, :], g_ref[...]).astype(BF16))
        pbuf_ref[POOL_HALO + rows.start:POOL_HALO + rows.stop, :] = proj(us[-1], 0, pool_w)
    for rows, u in zip(tiles, us):
        q_ref[rows, :] = (proj(u, o1, qk_w) * (LOG2E / math.sqrt(HEAD_DIM))).astype(BF16)
        k_ref[rows, :] = proj(u, o2, qk_w).astype(BF16)

    t_loc = seq_tile * tm + lax.broadcasted_iota(jnp.int32, (tm, 1), 0)
    for g, w in enumerate(POOL_WINDOWS):
        cols = slice(g * POOL_GROUP_DIM, (g + 1) * POOL_GROUP_DIM)
        run = pbuf_ref[:, cols]
        d = 1
        while d < w:
            run = run + pltpu.roll(run, d, 0)
            d *= 2
        tok = pbuf_ref[POOL_HALO:POOL_HALO + tm, cols]
        cnt = jnp.minimum(t_loc + 1, w).astype(F32)
        pooled = run[POOL_HALO:, :] / cnt - tok
        y = jnp.dot(pooled.astype(BF16), pm_ref[g], preferred_element_type=F32)
        ybuf_ref[:, cols] = (y * ps_ref[:, cols]).astype(BF16)

    for rows, u in zip(tiles, us):
        v_ref[rows, :] = proj(u, o3, v_w).astype(BF16)
        sgb_ref[rows, :] = jax.nn.sigmoid(proj(u, o5, d_model)).astype(BF16)
    for rows, u in zip(tiles, us):
        ya = jnp.dot(ybuf_ref[rows, :], wa_ref[...], preferred_element_type=F32)
        ma_ref[rows, :] = (jax.nn.sigmoid(proj(u, o4, d_model)) * ya).astype(BF16)


def _proj_call(x2, mix_pre_g, w_in, pool_mix, pool_scale, w_branch_a, later_weights,
               *, seq, tm, sub):
    t, d_model = x2.shape
    in_w = w_in.shape[1]
    pool_w = pool_scale.shape[1]
    v_w = pool_w
    qk_w = (in_w - pool_w - v_w - 2 * d_model) // 2
    steps = t // tm
    const = lambda i: (0, 0)
    row = lambda i: (i, 0)
    single = pl.Buffered(1)
    cast_steps, slab_in, slab_out, cast_shapes = [], [], [], []
    for w, layer in later_weights:
        _, rows_w, cols_w = w.shape
        n = max(d for d in range(1, steps + 1) if rows_w % (d * BF16_SUBLANES) == 0)
        block = (rows_w // n, cols_w)
        cast_steps.append(n)
        slab_in.append(pl.BlockSpec(
            (None,) + block, lambda i, n=n, layer=layer: (layer, jnp.minimum(i, n - 1), 0)))
        slab_out.append(pl.BlockSpec(block, lambda i, n=n: (jnp.minimum(i, n - 1), 0)))
        cast_shapes.append(jax.ShapeDtypeStruct((rows_w, cols_w), BF16))
    kern = functools.partial(_proj_kernel, cast_steps=tuple(cast_steps), tm=tm, sub=sub,
                             tiles_per_seq=seq // tm, pool_w=pool_w, qk_w=qk_w, v_w=v_w,
                             d_model=d_model)
    outs = pl.pallas_call(
        kern,
        grid=(steps,),
        in_specs=[
            pl.BlockSpec((tm, d_model), row),
            pl.BlockSpec((1, d_model), const),
            pl.BlockSpec((d_model, in_w), const, pipeline_mode=single),
            pl.BlockSpec(pool_mix.shape, lambda i: (0, 0, 0), pipeline_mode=single),
            pl.BlockSpec((1, pool_w), const),
            pl.BlockSpec((pool_w, d_model), const, pipeline_mode=single),
        ] + slab_in,
        out_specs=[
            pl.BlockSpec((tm, qk_w), row),
            pl.BlockSpec((tm, qk_w), row),
            pl.BlockSpec((tm, v_w), row),
            pl.BlockSpec((tm, d_model), row),
            pl.BlockSpec((tm, d_model), row),
        ] + slab_out,
        out_shape=[
            jax.ShapeDtypeStruct((t, qk_w), BF16),
            jax.ShapeDtypeStruct((t, qk_w), BF16),
            jax.ShapeDtypeStruct((t, v_w), BF16),
            jax.ShapeDtypeStruct((t, d_model), BF16),
            jax.ShapeDtypeStruct((t, d_model), BF16),
        ] + cast_shapes,
        scratch_shapes=[
            pltpu.VMEM((POOL_HALO + tm, pool_w), F32),
            pltpu.VMEM((tm, pool_w), BF16),
        ],
        compiler_params=pltpu.CompilerParams(
            dimension_semantics=("arbitrary",), vmem_limit_bytes=VMEM_LIMIT_BYTES),
        name="proj_pool",
    )(x2, mix_pre_g, w_in, pool_mix, pool_scale, w_branch_a, *[w for w, _ in later_weights])
    return outs[:5], outs[5:]


def _attn_kernel(slopes_ref, lq1_ref, lk1_ref, lq2_ref, lk2_ref, sg_ref,
                 q_ref, k_ref, v_ref, o_ref, kaug_ref, vaug_ref, acc1_ref, acc2_ref, qx_ref,
                 *, blk, rc, lambda_init):
    h = pl.program_id(1)
    qi = pl.program_id(2)
    slope = slopes_ref[h]
    s_len = k_ref.shape[0]

    rate = slope * LOG2E
    lane_row = lax.broadcasted_iota(jnp.int32, (1, AUG_LANES), 1)
    part = jnp.where(lane_row >= N_BIAS_LANES, lane_row - N_BIAS_LANES, lane_row) >> 1
    r0 = jnp.full((1, AUG_LANES), rate, F32)
    r_hi = r0.astype(BF16).astype(F32)
    r_mid = (r0 - r_hi).astype(BF16).astype(F32)
    r_lo = (r0 - r_hi - r_mid).astype(BF16).astype(F32)
    rate_lanes = jnp.where(part == 0, r_hi, jnp.where(part == 1, r_mid, r_lo))

    def aug_lanes(idx, lane, negate_pos):
        lo = idx & (BF16_EXACT_INT - 1)
        pos = jnp.where((lane & 1) == 0, idx - lo, lo).astype(F32)
        first = lane < N_BIAS_LANES
        second = jnp.logical_and(lane >= N_BIAS_LANES, lane < 2 * N_BIAS_LANES)
        if negate_pos:
            x = jnp.where(first, -pos, jnp.where(second, rate_lanes, 0.0))
        else:
            x = jnp.where(first, rate_lanes, jnp.where(second, pos, 0.0))
        return x.astype(BF16)

    @pl.when(qi == 0)
    def _():
        period = 2 * blk
        lane = lax.broadcasted_iota(jnp.int32, (period, AUG_LANES), 1)
        kx = aug_lanes(lax.broadcasted_iota(jnp.int32, (period, AUG_LANES), 0), lane, False)
        vx = jnp.where(lane == 0, 1.0, 0.0).astype(BF16)
        kaug_ref[:, 0:HEAD_WIDTH] = k_ref[...]
        vaug_ref[:, 0:HEAD_WIDTH] = v_ref[...]
        for r0 in range(0, s_len, period):
            kaug_ref[r0:r0 + period, HEAD_WIDTH:] = kx
            vaug_ref[r0:r0 + period, HEAD_WIDTH:] = vx
        qx_ref[...] = aug_lanes(lax.broadcasted_iota(jnp.int32, (blk, AUG_LANES), 0),
                                lax.broadcasted_iota(jnp.int32, (blk, AUG_LANES), 1), True)

    lam = (jnp.exp(jnp.sum(lq1_ref[...] * lk1_ref[...], axis=-1, keepdims=True))
           - jnp.exp(jnp.sum(lq2_ref[...] * lk2_ref[...], axis=-1, keepdims=True))
           + lambda_init)

    q = q_ref[...]
    lane = lax.broadcasted_iota(jnp.int32, q.shape, 1)
    qx = qx_ref[...]
    zero = jnp.zeros_like(q)
    q1 = jnp.concatenate([jnp.where(lane < HEAD_DIM, q, zero), qx], axis=1)
    q2 = jnp.concatenate([jnp.where(lane >= HEAD_DIM, q, zero), qx], axis=1)

    acc1_ref[...] = jnp.zeros_like(acc1_ref)
    acc2_ref[...] = jnp.zeros_like(acc2_ref)

    maps = ((q1, acc1_ref), (q2, acc2_ref))

    lower_tri = (lax.broadcasted_iota(jnp.int32, (rc, rc), 0)
                 >= lax.broadcasted_iota(jnp.int32, (rc, rc), 1))

    def scores(qr, rows, kstart, nk, diagonal=False):
        kb = kaug_ref[pl.ds(kstart, nk), :]
        nt = (((1,), (1,)), ((), ()))
        s = lax.dot_general(qr[rows, :], kb, nt, preferred_element_type=F32)
        if diagonal:
            masked = jnp.where(lower_tri, s[:, nk - rc:], MASK_VALUE)
            s = masked if nk == rc else jnp.concatenate([s[:, :nk - rc], masked], axis=1)
        return s

    def update(s, ar, rows, kstart, nk, m, c):
        vb = vaug_ref[pl.ds(kstart, nk), :]
        m_new = jnp.maximum(m, jnp.max(s, axis=-1, keepdims=True) + c)
        alpha = jnp.exp2(m - m_new)
        p = jnp.exp2(s - (m_new - c))
        ar[rows, :] = alpha * ar[rows, :] + jnp.dot(p.astype(BF16), vb,
                                                    preferred_element_type=F32)
        return m_new

    every = slice(None)

    def bias_const(kj):
        return -rate * (qi * blk - (kj >> 1) * (2 * blk)).astype(F32)

    def block_pair(t, ms):
        start = t * (2 * blk)
        ss = [scores(qr, every, start, 2 * blk) for qr, _ in maps]
        return tuple(update(s, ar, every, start, 2 * blk, m, bias_const(2 * t))
                     for s, (_, ar), m in zip(ss, maps, ms))

    ms0 = tuple(jnp.full((blk, 1), MASK_VALUE, F32) for _ in maps)

    def tail(n_before, ms):
        start = pl.multiple_of(qi * blk - n_before, blk)
        chunks = [(slice(r * rc, (r + 1) * rc), n_before + (r + 1) * rc)
                  for r in range(blk // rc)]
        ss = [[scores(qr, rows, start, nk, diagonal=True) for qr, _ in maps]
              for rows, nk in chunks]
        for (rows, nk), s_pair in zip(chunks, ss):
            for s, (_, ar), m in zip(s_pair, maps, ms):
                update(s, ar, rows, start, nk, m[rows], bias_const(qi))
            a1 = acc1_ref[rows, :]
            a2 = acc2_ref[rows, :]
            o = (a1[:, :HEAD_WIDTH] / a1[:, HEAD_WIDTH:HEAD_WIDTH + 1]
                 - lam * (a2[:, :HEAD_WIDTH] / a2[:, HEAD_WIDTH:HEAD_WIDTH + 1]))
            o_ref[rows, :] = (_rms(o, sg_ref[...]) * (1.0 - lambda_init)).astype(BF16)

    def whole_q_block(c):
        ms = ms0
        for t in range(c // 2):
            ms = block_pair(t, ms)
        tail(blk if c % 2 else 0, ms)

    for c in range(s_len // blk):
        pl.when(qi == c)(functools.partial(whole_q_block, c))


def _attn_call(q, k, v, slopes, lq1, lk1, lq2, lk2, subln_g, *, blk, rc, lambda_init):
    b, s, width = q.shape
    heads = width // HEAD_WIDTH
    small = lambda bi, hi, qi: (0, 0)
    kern = functools.partial(_attn_kernel, blk=blk, rc=rc, lambda_init=lambda_init)
    return pl.pallas_call(
        kern,
        grid=(b, heads, s // blk),
        in_specs=[
            pl.BlockSpec(memory_space=pltpu.SMEM),
            pl.BlockSpec((1, HEAD_DIM), small),
            pl.BlockSpec((1, HEAD_DIM), small),
            pl.BlockSpec((1, HEAD_DIM), small),
            pl.BlockSpec((1, HEAD_DIM), small),
            pl.BlockSpec((1, HEAD_WIDTH), small),
            pl.BlockSpec((None, blk, HEAD_WIDTH), lambda bi, hi, qi: (bi, qi, hi)),
            pl.BlockSpec((None, s, HEAD_WIDTH), lambda bi, hi, qi: (bi, 0, hi)),
            pl.BlockSpec((None, s, HEAD_WIDTH), lambda bi, hi, qi: (bi, 0, hi)),
        ],
        out_specs=pl.BlockSpec((None, blk, HEAD_WIDTH), lambda bi, hi, qi: (bi, qi, hi)),
        out_shape=jax.ShapeDtypeStruct((b, s, width), BF16),
        scratch_shapes=[
            pltpu.VMEM((s, HEAD_WIDTH + AUG_LANES), BF16),
            pltpu.VMEM((s, HEAD_WIDTH + AUG_LANES), BF16),
            pltpu.VMEM((blk, HEAD_WIDTH + AUG_LANES), F32),
            pltpu.VMEM((blk, HEAD_WIDTH + AUG_LANES), F32),
            pltpu.VMEM((blk, AUG_LANES), BF16),
        ],
        compiler_params=pltpu.CompilerParams(
            dimension_semantics=("parallel", "parallel", "arbitrary"),
            vmem_limit_bytes=VMEM_LIMIT_BYTES),
        name="diff_attn",
    )(slopes, lq1, lk1, lq2, lk2, subln_g, q, k, v)


def _out_ffn_kernel(x_ref, o_ref, ma_ref, sgb_ref, wb_ref, wo_ref, g_post_ref,
                    g_pre_ref, g_fpost_ref, wg_ref, wu_ref, wd_ref, out_ref, *, sub):
    tiles = [slice(r0, r0 + sub) for r0 in range(0, x_ref.shape[0], sub)]

    def mix(rows):
        yb = jnp.dot(o_ref[rows, :], wb_ref[...], preferred_element_type=F32)
        m = ma_ref[rows, :].astype(F32) + sgb_ref[rows, :].astype(F32) * yb
        return jnp.dot(m.astype(BF16), wo_ref[...], preferred_element_type=F32)

    def norms(rows, mo):
        h1 = x_ref[rows, :] + _rms(mo, g_post_ref[...])
        return h1, _rms(h1, g_pre_ref[...]).astype(BF16)

    def hidden(u):
        gate = jnp.dot(u, wg_ref[...], preferred_element_type=F32)
        up = jnp.dot(u, wu_ref[...], preferred_element_type=F32)
        return (jax.nn.silu(gate) * up).astype(BF16)

    mos = [mix(rows) for rows in tiles]
    hus = [norms(rows, mo) for rows, mo in zip(tiles, mos)]
    fs = [hidden(u) for _, u in hus]
    dns = [jnp.dot(f, wd_ref[...], preferred_element_type=F32) for f in fs]
    for rows, (h1, _), dn in zip(tiles, hus, dns):
        out_ref[rows, :] = h1 + _rms(dn, g_fpost_ref[...])


def _out_ffn_call(x2, o2, ma, sgb, w_branch_b, w_out, mix_post_g, ffn_pre_g, ffn_post_g,
                  w_gate, w_up, w_down, *, tm, sub):
    t, d_model = x2.shape
    const = lambda i: (0, 0)
    row = lambda i: (i, 0)
    single = pl.Buffered(1)

    def wspec(w):
        return pl.BlockSpec(w.shape, const, pipeline_mode=single)

    return pl.pallas_call(
        functools.partial(_out_ffn_kernel, sub=sub),
        grid=(t // tm,),
        in_specs=[
            pl.BlockSpec((tm, d_model), row),
            pl.BlockSpec((tm, o2.shape[1]), row),
            pl.BlockSpec((tm, d_model), row),
            pl.BlockSpec((tm, d_model), row),
            wspec(w_branch_b),
            wspec(w_out),
            pl.BlockSpec((1, d_model), const),
            pl.BlockSpec((1, d_model), const),
            pl.BlockSpec((1, d_model), const),
            wspec(w_gate),
            wspec(w_up),
            wspec(w_down),
        ],
        out_specs=pl.BlockSpec((tm, d_model), row),
        out_shape=jax.ShapeDtypeStruct((t, d_model), F32),
        compiler_params=pltpu.CompilerParams(
            dimension_semantics=("parallel",), vmem_limit_bytes=VMEM_LIMIT_BYTES),
        name="out_ffn",
    )(x2, o2, ma, sgb, w_branch_b, w_out, mix_post_g, ffn_pre_g, ffn_post_g,
      w_gate, w_up, w_down)


def _alibi_slopes(n):
    start = 2.0 ** (-8.0 / n)
    return jnp.asarray([start ** (i + 1) for i in range(n)], dtype=F32)


def _layer(h, layer_idx, w_in, pool_mix, pool_scale, w_branch_a, lam_q1, lam_k1, lam_q2,
           lam_k2, subln_g, w_branch_b, w_out, mix_pre_g, mix_post_g, ffn_pre_g, ffn_post_g,
           w_ffn_gate, w_ffn_up, w_ffn_down):
    b, s, d_model = h.shape
    lambda_init = 0.8 - 0.6 * math.exp(-0.3 * layer_idx)
    x2 = h.reshape(b * s, d_model)
    tiles = _tiling(s)
    row = lambda a: a.reshape(1, -1)

    (q, k, v, ma, sgb), (wb16, wo16, wg16, wu16, wd16) = _proj_call(
        x2, row(mix_pre_g), w_in.astype(BF16), pool_mix.astype(BF16), row(pool_scale),
        w_branch_a.astype(BF16),
        [(w, layer_idx) for w in (w_branch_b, w_out, w_ffn_gate, w_ffn_up, w_ffn_down)],
        seq=s, tm=tiles["proj_tm"], sub=tiles["sub"])

    width = q.shape[1]
    heads = width // HEAD_WIDTH
    o = _attn_call(q.reshape(b, s, width), k.reshape(b, s, width), v.reshape(b, s, width),
                   _alibi_slopes(heads), row(lam_q1), row(lam_k1), row(lam_q2), row(lam_k2),
                   row(subln_g), blk=tiles["attn_blk"], rc=tiles["attn_rc"],
                   lambda_init=lambda_init)

    out = _out_ffn_call(
        x2, o.reshape(b * s, width), ma, sgb, wb16, wo16,
        row(mix_post_g), row(ffn_pre_g), row(ffn_post_g), wg16, wu16, wd16,
        tm=tiles["ffn_tm"], sub=tiles["sub"])
    return out.reshape(b, s, d_model)


def kernel(x, w_in, pool_mix, pool_scale, w_branch_a, lam_q1, lam_k1, lam_q2, lam_k2,
           subln_g, w_branch_b, w_out, mix_pre_g, mix_post_g, ffn_pre_g, ffn_post_g,
           w_ffn_gate, w_ffn_up, w_ffn_down):
    h = x
    for l in range(w_in.shape[0]):
        h = _layer(h, l, w_in[l], pool_mix[l], pool_scale[l], w_branch_a[l], lam_q1[l],
                   lam_k1[l], lam_q2[l], lam_k2[l], subln_g[l], w_branch_b, w_out,
                   mix_pre_g[l], mix_post_g[l], ffn_pre_g[l], ffn_post_g[l], w_ffn_gate,
                   w_ffn_up, w_ffn_down)
    return h
```

```python
import functools
import math

import jax
import jax.numpy as jnp
from jax import lax
from jax.experimental import pallas as pl
from jax.experimental.pallas import tpu as pltpu

NORM_EPS = 1e-6
POOL_WINDOWS = (2, 4, 8, 16)
POOL_GROUP_DIM = 128
POOL_HALO = 16
HEAD_DIM = 64
HEAD_WIDTH = 2 * HEAD_DIM
MASK_VALUE = -1e30
AUG_LANES = 128
BF16_EXACT_INT = 256
BF16_SUBLANES = 16
N_BIAS_LANES = 6
LOG2E = math.log2(math.e)
BF16 = jnp.bfloat16
F32 = jnp.float32

VMEM_LIMIT_BYTES = 52 * 1024 * 1024
MXU_ROWS = 256


def _tiling(seq):
    proj_tm = 4 * MXU_ROWS
    attn_blk = 4 * MXU_ROWS
    ffn_tm = 2 * MXU_ROWS
    assert seq % proj_tm == 0 and seq % (2 * attn_blk) == 0 and seq % ffn_tm == 0
    assert attn_blk >= BF16_EXACT_INT and AUG_LANES == HEAD_WIDTH
    return dict(proj_tm=proj_tm, attn_blk=attn_blk, attn_rc=MXU_ROWS, ffn_tm=ffn_tm,
                sub=MXU_ROWS)


def _rms(x, g):
    ms = jnp.mean(x * x, axis=-1, keepdims=True)
    return x * lax.rsqrt(ms + NORM_EPS) * g


def _proj_kernel(x_ref, g_ref, w_ref, pm_ref, ps_ref, wa_ref, *rest,
                 cast_steps, tm, sub, tiles_per_seq, pool_w, qk_w, v_w, d_model):
    n_cast = len(cast_steps)
    cast_in = rest[:n_cast]
    q_ref, k_ref, v_ref, ma_ref, sgb_ref = rest[n_cast:n_cast + 5]
    cast_out = rest[n_cast + 5:2 * n_cast + 5]
    pbuf_ref, ybuf_ref = rest[2 * n_cast + 5:]
    i = pl.program_id(0)
    seq_tile = i % tiles_per_seq
    for src, dst, n in zip(cast_in, cast_out, cast_steps):
        @pl.when(i < n)
        def _(src=src, dst=dst):
            dst[...] = src[...].astype(BF16)

    tiles = [slice(r0, r0 + sub) for r0 in range(0, tm, sub)]

    def proj(u, lo, width):
        return jnp.dot(u, w_ref[:, lo:lo + width], preferred_element_type=F32)

    o1 = pool_w
    o2 = o1 + qk_w
    o3 = o2 + qk_w
    o4 = o3 + v_w
    o5 = o4 + d_model
    @pl.when(seq_tile == 0)
    def _():
        pbuf_ref[0:POOL_HALO, :] = jnp.zeros((POOL_HALO, pool_w), F32)

    @pl.when(seq_tile != 0)
    def _():
        pbuf_ref[0:POOL_HALO, :] = pbuf_ref[tm:tm + POOL_HALO, :]

    us = []
    for rows in tiles:
        us.append(_rms(x_ref[rows, :], g_ref[...]).astype(BF16))
        pbuf_ref[POOL_HALO + rows.start:POOL_HALO + rows.stop, :] = proj(us[-1], 0, pool_w)
    for rows, u in zip(tiles, us):
        q_ref[rows, :] = (proj(u, o1, qk_w) * (LOG2E / math.sqrt(HEAD_DIM))).astype(BF16)
        k_ref[rows, :] = proj(u, o2, qk_w).astype(BF16)

    t_loc = seq_tile * tm + lax.broadcasted_iota(jnp.int32, (tm, 1), 0)
    for g, w in enumerate(POOL_WINDOWS):
        cols = slice(g * POOL_GROUP_DIM, (g + 1) * POOL_GROUP_DIM)
        run = pbuf_ref[:, cols]
        d = 1
        while d < w:
            run = run + pltpu.roll(run, d, 0)
            d *= 2
        tok = pbuf_ref[POOL_HALO:POOL_HALO + tm, cols]
        cnt = jnp.minimum(t_loc + 1, w).astype(F32)
        pooled = run[POOL_HALO:, :] / cnt - tok
        y = jnp.dot(pooled.astype(BF16), pm_ref[g], preferred_element_type=F32)
        ybuf_ref[:, cols] = (y * ps_ref[:, cols]).astype(BF16)

    for rows, u in zip(tiles, us):
        v_ref[rows, :] = proj(u, o3, v_w).astype(BF16)
        sgb_ref[rows, :] = jax.nn.sigmoid(proj(u, o5, d_model)).astype(BF16)
    for rows, u in zip(tiles, us):
        ya = jnp.dot(ybuf_ref[rows, :], wa_ref[...], preferred_element_type=F32)
        ma_ref[rows, :] = (jax.nn.sigmoid(proj(u, o4, d_model)) * ya).astype(BF16)


def _proj_call(x2, mix_pre_g, w_in, pool_mix, pool_scale, w_branch_a, later_weights,
               *, seq, tm, sub):
    t, d_model = x2.shape
    in_w = w_in.shape[1]
    pool_w = pool_scale.shape[1]
    v_w = pool_w
    qk_w = (in_w - pool_w - v_w - 2 * d_model) // 2
    steps = t // tm
    const = lambda i: (0, 0)
    row = lambda i: (i, 0)
    single = pl.Buffered(1)
    cast_steps, slab_in, slab_out, cast_shapes = [], [], [], []
    for w, layer in later_weights:
        _, rows_w, cols_w = w.shape
        n = max(d for d in range(1, steps + 1) if rows_w % (d * BF16_SUBLANES) == 0)
        block = (rows_w // n, cols_w)
        cast_steps.append(n)
        slab_in.append(pl.BlockSpec(
            (None,) + block, lambda i, n=n, layer=layer: (layer, jnp.minimum(i, n - 1), 0)))
        slab_out.append(pl.BlockSpec(block, lambda i, n=n: (jnp.minimum(i, n - 1), 0)))
        cast_shapes.append(jax.ShapeDtypeStruct((rows_w, cols_w), BF16))
    kern = functools.partial(_proj_kernel, cast_steps=tuple(cast_steps), tm=tm, sub=sub,
                             tiles_per_seq=seq // tm, pool_w=pool_w, qk_w=qk_w, v_w=v_w,
                             d_model=d_model)
    outs = pl.pallas_call(
        kern,
        grid=(steps,),
        in_specs=[
            pl.BlockSpec((tm, d_model), row),
            pl.BlockSpec((1, d_model), const),
            pl.BlockSpec((d_model, in_w), const, pipeline_mode=single),
            pl.BlockSpec(pool_mix.shape, lambda i: (0, 0, 0), pipeline_mode=single),
            pl.BlockSpec((1, pool_w), const),
            pl.BlockSpec((pool_w, d_model), const, pipeline_mode=single),
        ] + slab_in,
        out_specs=[
            pl.BlockSpec((tm, qk_w), row),
            pl.BlockSpec((tm, qk_w), row),
            pl.BlockSpec((tm, v_w), row),
            pl.BlockSpec((tm, d_model), row),
            pl.BlockSpec((tm, d_model), row),
        ] + slab_out,
        out_shape=[
            jax.ShapeDtypeStruct((t, qk_w), BF16),
            jax.ShapeDtypeStruct((t, qk_w), BF16),
            jax.ShapeDtypeStruct((t, v_w), BF16),
            jax.ShapeDtypeStruct((t, d_model), BF16),
            jax.ShapeDtypeStruct((t, d_model), BF16),
        ] + cast_shapes,
        scratch_shapes=[
            pltpu.VMEM((POOL_HALO + tm, pool_w), F32),
            pltpu.VMEM((tm, pool_w), BF16),
        ],
        compiler_params=pltpu.CompilerParams(
            dimension_semantics=("arbitrary",), vmem_limit_bytes=VMEM_LIMIT_BYTES),
        name="proj_pool",
    )(x2, mix_pre_g, w_in, pool_mix, pool_scale, w_branch_a, *[w for w, _ in later_weights])
    return outs[:5], outs[5:]


def _attn_kernel(slopes_ref, lq1_ref, lk1_ref, lq2_ref, lk2_ref, sg_ref,
                 q_ref, k_ref, v_ref, o_ref, kaug_ref, vaug_ref, acc1_ref, acc2_ref, qx_ref,
                 *, blk, rc, lambda_init):
    h = pl.program_id(1)
    qi = pl.program_id(2)
    slope = slopes_ref[h]
    s_len = k_ref.shape[0]

    rate = slope * LOG2E
    lane_row = lax.broadcasted_iota(jnp.int32, (1, AUG_LANES), 1)
    part = jnp.where(lane_row >= N_BIAS_LANES, lane_row - N_BIAS_LANES, lane_row) >> 1
    r0 = jnp.full((1, AUG_LANES), rate, F32)
    r_hi = r0.astype(BF16).astype(F32)
    r_mid = (r0 - r_hi).astype(BF16).astype(F32)
    r_lo = (r0 - r_hi - r_mid).astype(BF16).astype(F32)
    rate_lanes = jnp.where(part == 0, r_hi, jnp.where(part == 1, r_mid, r_lo))

    def aug_lanes(idx, lane, negate_pos):
        lo = idx & (BF16_EXACT_INT - 1)
        pos = jnp.where((lane & 1) == 0, idx - lo, lo).astype(F32)
        first = lane < N_BIAS_LANES
        second = jnp.logical_and(lane >= N_BIAS_LANES, lane < 2 * N_BIAS_LANES)
        if negate_pos:
            x = jnp.where(first, -pos, jnp.where(second, rate_lanes, 0.0))
        else:
            x = jnp.where(first, rate_lanes, jnp.where(second, pos, 0.0))
        return x.astype(BF16)

    @pl.when(qi == 0)
    def _():
        period = 2 * blk
        lane = lax.broadcasted_iota(jnp.int32, (period, AUG_LANES), 1)
        kx = aug_lanes(lax.broadcasted_iota(jnp.int32, (period, AUG_LANES), 0), lane, False)
        vx = jnp.where(lane == 0, 1.0, 0.0).astype(BF16)
        kaug_ref[:, 0:HEAD_WIDTH] = k_ref[...]
        vaug_ref[:, 0:HEAD_WIDTH] = v_ref[...]
        for r0 in range(0, s_len, period):
            kaug_ref[r0:r0 + period, HEAD_WIDTH:] = kx
            vaug_ref[r0:r0 + period, HEAD_WIDTH:] = vx
        qx_ref[...] = aug_lanes(lax.broadcasted_iota(jnp.int32, (blk, AUG_LANES), 0),
                                lax.broadcasted_iota(jnp.int32, (blk, AUG_LANES), 1), True)

    lam = (jnp.exp(jnp.sum(lq1_ref[...] * lk1_ref[...], axis=-1, keepdims=True))
           - jnp.exp(jnp.sum(lq2_ref[...] * lk2_ref[...], axis=-1, keepdims=True))
           + lambda_init)

    q = q_ref[...]
    lane = lax.broadcasted_iota(jnp.int32, q.shape, 1)
    qx = qx_ref[...]
    zero = jnp.zeros_like(q)
    q1 = jnp.concatenate([jnp.where(lane < HEAD_DIM, q, zero), qx], axis=1)
    q2 = jnp.concatenate([jnp.where(lane >= HEAD_DIM, q, zero), qx], axis=1)

    acc1_ref[...] = jnp.zeros_like(acc1_ref)
    acc2_ref[...] = jnp.zeros_like(acc2_ref)

    maps = ((q1, acc1_ref), (q2, acc2_ref))

    lower_tri = (lax.broadcasted_iota(jnp.int32, (rc, rc), 0)
                 >= lax.broadcasted_iota(jnp.int32, (rc, rc), 1))

    def scores(qr, rows, kstart, nk, diagonal=False):
        kb = kaug_ref[pl.ds(kstart, nk), :]
        nt = (((1,), (1,)), ((), ()))
        s = lax.dot_general(qr[rows, :], kb, nt, preferred_element_type=F32)
        if diagonal:
            masked = jnp.where(lower_tri, s[:, nk - rc:], MASK_VALUE)
            s = masked if nk == rc else jnp.concatenate([s[:, :nk - rc], masked], axis=1)
        return s

    def update(s, ar, rows, kstart, nk, m, c):
        vb = vaug_ref[pl.ds(kstart, nk), :]
        m_new = jnp.maximum(m, jnp.max(s, axis=-1, keepdims=True) + c)
        alpha = jnp.exp2(m - m_new)
        p = jnp.exp2(s - (m_new - c))
        ar[rows, :] = alpha * ar[rows, :] + jnp.dot(p.astype(BF16), vb,
                                                    preferred_element_type=F32)
        return m_new

    every = slice(None)

    def bias_const(kj):
        return -rate * (qi * blk - (kj >> 1) * (2 * blk)).astype(F32)

    def block_pair(t, ms):
        start = pl.multiple_of(t * (2 * blk), 2 * blk)
        ss = [scores(qr, every, start, 2 * blk) for qr, _ in maps]
        return tuple(update(s, ar, every, start, 2 * blk, m, bias_const(2 * t))
                     for s, (_, ar), m in zip(ss, maps, ms))

    ms0 = tuple(jnp.full((blk, 1), MASK_VALUE, F32) for _ in maps)

    def tail(n_before, ms):
        start = pl.multiple_of(qi * blk - n_before, blk)
        chunks = [(slice(r * rc, (r + 1) * rc), n_before + (r + 1) * rc)
                  for r in range(blk // rc)]
        ss = [[scores(qr, rows, start, nk, diagonal=True) for qr, _ in maps]
              for rows, nk in chunks]
        for (rows, nk), s_pair in zip(chunks, ss):
            for s, (_, ar), m in zip(s_pair, maps, ms):
                update(s, ar, rows, start, nk, m[rows], bias_const(qi))
            a1 = acc1_ref[rows, :]
            a2 = acc2_ref[rows, :]
            o = (a1[:, :HEAD_WIDTH] / a1[:, HEAD_WIDTH:HEAD_WIDTH + 1]
                 - lam * (a2[:, :HEAD_WIDTH] / a2[:, HEAD_WIDTH:HEAD_WIDTH + 1]))
            o_ref[rows, :] = (_rms(o, sg_ref[...]) * (1.0 - lambda_init)).astype(BF16)

    n_pairs = qi // 2
    ms = lax.fori_loop(0, jnp.maximum(n_pairs - 1, 0), block_pair, ms0)

    def last_pair_and_tail(n_before):
        tail(n_before, block_pair(n_pairs - 1, ms))

    odd = qi % 2 == 1
    has_pair = n_pairs > 0
    pl.when(jnp.logical_and(has_pair, odd))(lambda: last_pair_and_tail(blk))
    pl.when(jnp.logical_and(has_pair, jnp.logical_not(odd)))(lambda: last_pair_and_tail(0))
    pl.when(jnp.logical_and(jnp.logical_not(has_pair), odd))(lambda: tail(blk, ms))
    pl.when(jnp.logical_and(jnp.logical_not(has_pair), jnp.logical_not(odd)))(
        lambda: tail(0, ms))


def _attn_call(q, k, v, slopes, lq1, lk1, lq2, lk2, subln_g, *, blk, rc, lambda_init):
    b, s, width = q.shape
    heads = width // HEAD_WIDTH
    small = lambda bi, hi, qi: (0, 0)
    kern = functools.partial(_attn_kernel, blk=blk, rc=rc, lambda_init=lambda_init)
    return pl.pallas_call(
        kern,
        grid=(b, heads, s // blk),
        in_specs=[
            pl.BlockSpec(memory_space=pltpu.SMEM),
            pl.BlockSpec((1, HEAD_DIM), small),
            pl.BlockSpec((1, HEAD_DIM), small),
            pl.BlockSpec((1, HEAD_DIM), small),
            pl.BlockSpec((1, HEAD_DIM), small),
            pl.BlockSpec((1, HEAD_WIDTH), small),
            pl.BlockSpec((None, blk, HEAD_WIDTH), lambda bi, hi, qi: (bi, qi, hi)),
            pl.BlockSpec((None, s, HEAD_WIDTH), lambda bi, hi, qi: (bi, 0, hi)),
            pl.BlockSpec((None, s, HEAD_WIDTH), lambda bi, hi, qi: (bi, 0, hi)),
        ],
        out_specs=pl.BlockSpec((None, blk, HEAD_WIDTH), lambda bi, hi, qi: (bi, qi, hi)),
        out_shape=jax.ShapeDtypeStruct((b, s, width), BF16),
        scratch_shapes=[
            pltpu.VMEM((s, HEAD_WIDTH + AUG_LANES), BF16),
            pltpu.VMEM((s, HEAD_WIDTH + AUG_LANES), BF16),
            pltpu.VMEM((blk, HEAD_WIDTH + AUG_LANES), F32),
            pltpu.VMEM((blk, HEAD_WIDTH + AUG_LANES), F32),
            pltpu.VMEM((blk, AUG_LANES), BF16),
        ],
        compiler_params=pltpu.CompilerParams(
            dimension_semantics=("parallel", "parallel", "arbitrary"),
            vmem_limit_bytes=VMEM_LIMIT_BYTES),
        name="diff_attn",
    )(slopes, lq1, lk1, lq2, lk2, subln_g, q, k, v)


def _out_ffn_kernel(x_ref, o_ref, ma_ref, sgb_ref, wb_ref, wo_ref, g_post_ref,
                    g_pre_ref, g_fpost_ref, wg_ref, wu_ref, wd_ref, out_ref, *, sub):
    tiles = [slice(r0, r0 + sub) for r0 in range(0, x_ref.shape[0], sub)]

    def mix(rows):
        yb = jnp.dot(o_ref[rows, :], wb_ref[...], preferred_element_type=F32)
        m = ma_ref[rows, :].astype(F32) + sgb_ref[rows, :].astype(F32) * yb
        return jnp.dot(m.astype(BF16), wo_ref[...], preferred_element_type=F32)

    def norms(rows, mo):
        h1 = x_ref[rows, :] + _rms(mo, g_post_ref[...])
        return h1, _rms(h1, g_pre_ref[...]).astype(BF16)

    def hidden(u):
        gate = jnp.dot(u, wg_ref[...], preferred_element_type=F32)
        up = jnp.dot(u, wu_ref[...], preferred_element_type=F32)
        return (jax.nn.silu(gate) * up).astype(BF16)

    mos = [mix(rows) for rows in tiles]
    hus = [norms(rows, mo) for rows, mo in zip(tiles, mos)]
    fs = [hidden(u) for _, u in hus]
    dns = [jnp.dot(f, wd_ref[...], preferred_element_type=F32) for f in fs]
    for rows, (h1, _), dn in zip(tiles, hus, dns):
        out_ref[rows, :] = h1 + _rms(dn, g_fpost_ref[...])


def _out_ffn_call(x2, o2, ma, sgb, w_branch_b, w_out, mix_post_g, ffn_pre_g, ffn_post_g,
                  w_gate, w_up, w_down, *, tm, sub):
    t, d_model = x2.shape
    const = lambda i: (0, 0)
    row = lambda i: (i, 0)
    single = pl.Buffered(1)

    def wspec(w):
        return pl.BlockSpec(w.shape, const, pipeline_mode=single)

    return pl.pallas_call(
        functools.partial(_out_ffn_kernel, sub=sub),
        grid=(t // tm,),
        in_specs=[
            pl.BlockSpec((tm, d_model), row),
            pl.BlockSpec((tm, o2.shape[1]), row),
            pl.BlockSpec((tm, d_model), row),
            pl.BlockSpec((tm, d_model), row),
            wspec(w_branch_b),
            wspec(w_out),
            pl.BlockSpec((1, d_model), const),
            pl.BlockSpec((1, d_model), const),
            pl.BlockSpec((1, d_model), const),
            wspec(w_gate),
            wspec(w_up),
            wspec(w_down),
        ],
        out_specs=pl.BlockSpec((tm, d_model), row),
        out_shape=jax.ShapeDtypeStruct((t, d_model), F32),
        compiler_params=pltpu.CompilerParams(
            dimension_semantics=("parallel",), vmem_limit_bytes=VMEM_LIMIT_BYTES),
        name="out_ffn",
    )(x2, o2, ma, sgb, w_branch_b, w_out, mix_post_g, ffn_pre_g, ffn_post_g,
      w_gate, w_up, w_down)


def _alibi_slopes(n):
    start = 2.0 ** (-8.0 / n)
    return jnp.asarray([start ** (i + 1) for i in range(n)], dtype=F32)


def _layer(h, layer_idx, w_in, pool_mix, pool_scale, w_branch_a, lam_q1, lam_k1, lam_q2,
           lam_k2, subln_g, w_branch_b, w_out, mix_pre_g, mix_post_g, ffn_pre_g, ffn_post_g,
           w_ffn_gate, w_ffn_up, w_ffn_down):
    b, s, d_model = h.shape
    lambda_init = 0.8 - 0.6 * math.exp(-0.3 * layer_idx)
    x2 = h.reshape(b * s, d_model)
    tiles = _tiling(s)
    row = lambda a: a.reshape(1, -1)

    (q, k, v, ma, sgb), (wb16, wo16, wg16, wu16, wd16) = _proj_call(
        x2, row(mix_pre_g), w_in.astype(BF16), pool_mix.astype(BF16), row(pool_scale),
        w_branch_a.astype(BF16),
        [(w, layer_idx) for w in (w_branch_b, w_out, w_ffn_gate, w_ffn_up, w_ffn_down)],
        seq=s, tm=tiles["proj_tm"], sub=tiles["sub"])

    width = q.shape[1]
    heads = width // HEAD_WIDTH
    o = _attn_call(q.reshape(b, s, width), k.reshape(b, s, width), v.reshape(b, s, width),
                   _alibi_slopes(heads), row(lam_q1), row(lam_k1), row(lam_q2), row(lam_k2),
                   row(subln_g), blk=tiles["attn_blk"], rc=tiles["attn_rc"],
                   lambda_init=lambda_init)

    out = _out_ffn_call(
        x2, o.reshape(b * s, width), ma, sgb, wb16, wo16,
        row(mix_post_g), row(ffn_pre_g), row(ffn_post_g), wg16, wu16, wd16,
        tm=tiles["ffn_tm"], sub=tiles["sub"])
    return out.reshape(b, s, d_model)


def kernel(x, w_in, pool_mix, pool_scale, w_branch_a, lam_q1, lam_k1, lam_q2, lam_k2,
           subln_g, w_branch_b, w_out, mix_pre_g, mix_post_g, ffn_pre_g, ffn_post_g,
           w_ffn_gate, w_ffn_up, w_ffn_down):
    h = x
    for l in range(w_in.shape[0]):
        h = _layer(h, l, w_in[l], pool_mix[l], pool_scale[l], w_branch_a[l], lam_q1[l],
                   lam_k1[l], lam_q2[l], lam_k2[l], subln_g[l], w_branch_b, w_out,
                   mix_pre_g[l], mix_post_g[l], ffn_pre_g[l], ffn_post_g[l], w_ffn_gate,
                   w_ffn_up, w_ffn_down)
    return h
```

```python
import functools
import math

import jax
import jax.numpy as jnp
from jax import lax
from jax.experimental import pallas as pl
from jax.experimental.pallas import tpu as pltpu

NORM_EPS = 1e-6
POOL_WINDOWS = (2, 4, 8, 16)
POOL_GROUP_DIM = 128
POOL_HALO = 16
HEAD_DIM = 64
HEAD_WIDTH = 2 * HEAD_DIM
MASK_VALUE = -1e30
AUG_LANES = 128
BF16_EXACT_INT = 256
BF16_SUBLANES = 16
N_BIAS_LANES = 6
LOG2E = math.log2(math.e)
BF16 = jnp.bfloat16
F32 = jnp.float32

VMEM_LIMIT_BYTES = 52 * 1024 * 1024
MXU_ROWS = 256


def _tiling(seq):
    proj_tm = 4 * MXU_ROWS
    attn_blk = 4 * MXU_ROWS
    ffn_tm = 2 * MXU_ROWS
    assert seq % proj_tm == 0 and seq % (2 * attn_blk) == 0 and seq % ffn_tm == 0
    assert attn_blk >= BF16_EXACT_INT and AUG_LANES == HEAD_WIDTH
    return dict(proj_tm=proj_tm, attn_blk=attn_blk, attn_rc=MXU_ROWS, ffn_tm=ffn_tm,
                sub=MXU_ROWS)


def _rms(x, g):
    ms = jnp.mean(x * x, axis=-1, keepdims=True)
    return x * lax.rsqrt(ms + NORM_EPS) * g


def _rate_lanes(slope):
    lane_row = lax.broadcasted_iota(jnp.int32, (1, AUG_LANES), 1)
    part = jnp.where(lane_row >= N_BIAS_LANES, lane_row - N_BIAS_LANES, lane_row) >> 1
    r0 = jnp.full((1, AUG_LANES), slope * LOG2E, F32)
    r_hi = r0.astype(BF16).astype(F32)
    r_mid = (r0 - r_hi).astype(BF16).astype(F32)
    r_lo = (r0 - r_hi - r_mid).astype(BF16).astype(F32)
    return jnp.where(part == 0, r_hi, jnp.where(part == 1, r_mid, r_lo))


def _aug_lanes(idx, lane, rate_lanes, negate_pos):
    lo = idx & (BF16_EXACT_INT - 1)
    pos = jnp.where((lane & 1) == 0, idx - lo, lo).astype(F32)
    first = lane < N_BIAS_LANES
    second = jnp.logical_and(lane >= N_BIAS_LANES, lane < 2 * N_BIAS_LANES)
    if negate_pos:
        x = jnp.where(first, -pos, jnp.where(second, rate_lanes, 0.0))
    else:
        x = jnp.where(first, rate_lanes, jnp.where(second, pos, 0.0))
    return x.astype(BF16)


def _proj_kernel(slopes_ref, x_ref, g_ref, w_ref, pm_ref, ps_ref, wa_ref, *rest,
                 cast_steps, tm, sub, tiles_per_seq, pool_w, qk_w, v_w, d_model, bias_period):
    n_cast = len(cast_steps)
    cast_in = rest[:n_cast]
    q_ref, k_ref, v_ref, ma_ref, sgb_ref = rest[n_cast:n_cast + 5]
    cast_out = rest[n_cast + 5:2 * n_cast + 5]
    pbuf_ref, ybuf_ref, kx_ref = rest[2 * n_cast + 5:]
    i = pl.program_id(0)
    seq_tile = i % tiles_per_seq
    for n in sorted(set(cast_steps)):
        @pl.when(i < n)
        def _(n=n):
            for src, dst, n_w in zip(cast_in, cast_out, cast_steps):
                if n_w == n:
                    dst[...] = src[...].astype(BF16)

    tiles = [slice(r0, r0 + sub) for r0 in range(0, tm, sub)]

    def proj(u, lo, width):
        return jnp.dot(u, w_ref[:, lo:lo + width], preferred_element_type=F32)

    n_heads = qk_w // HEAD_WIDTH
    tiles_per_period = bias_period // tm

    @pl.when(i == 0)
    def _():
        lane = lax.broadcasted_iota(jnp.int32, (tm, AUG_LANES), 1)
        row = lax.broadcasted_iota(jnp.int32, (tm, AUG_LANES), 0)
        for h in range(n_heads):
            rate = _rate_lanes(slopes_ref[h])
            for p in range(tiles_per_period):
                kx_ref[p, :, h * AUG_LANES:(h + 1) * AUG_LANES] = _aug_lanes(
                    row + p * tm, lane, rate, False)

    phase = seq_tile % tiles_per_period

    def store_keys(rows, data):
        for h in range(n_heads):
            c0 = h * (HEAD_WIDTH + AUG_LANES)
            k_ref[rows, c0:c0 + HEAD_WIDTH] = data[:, h * HEAD_WIDTH:(h + 1) * HEAD_WIDTH]
            k_ref[rows, c0 + HEAD_WIDTH:c0 + HEAD_WIDTH + AUG_LANES] = kx_ref[
                phase, rows, h * AUG_LANES:(h + 1) * AUG_LANES]

    o1 = pool_w
    o2 = o1 + qk_w
    o3 = o2 + qk_w
    o4 = o3 + v_w
    o5 = o4 + d_model
    @pl.when(seq_tile == 0)
    def _():
        pbuf_ref[0:POOL_HALO, :] = jnp.zeros((POOL_HALO, pool_w), F32)

    @pl.when(seq_tile != 0)
    def _():
        pbuf_ref[0:POOL_HALO, :] = pbuf_ref[tm:tm + POOL_HALO, :]

    us = []
    for rows in tiles:
        us.append(_rms(x_ref[rows, :], g_ref[...]).astype(BF16))
        pbuf_ref[POOL_HALO + rows.start:POOL_HALO + rows.stop, :] = proj(us[-1], 0, pool_w)
    for rows, u in zip(tiles, us):
        q_ref[rows, :] = (proj(u, o1, qk_w) * (LOG2E / math.sqrt(HEAD_DIM))).astype(BF16)
        store_keys(rows, proj(u, o2, qk_w).astype(BF16))

    t_loc = seq_tile * tm + lax.broadcasted_iota(jnp.int32, (tm, 1), 0)
    for g, w in enumerate(POOL_WINDOWS):
        cols = slice(g * POOL_GROUP_DIM, (g + 1) * POOL_GROUP_DIM)
        run = pbuf_ref[:, cols]
        d = 1
        while d < w:
            run = run + pltpu.roll(run, d, 0)
            d *= 2
        tok = pbuf_ref[POOL_HALO:POOL_HALO + tm, cols]
        cnt = jnp.minimum(t_loc + 1, w).astype(F32)
        pooled = run[POOL_HALO:, :] / cnt - tok
        y = jnp.dot(pooled.astype(BF16), pm_ref[g], preferred_element_type=F32)
        ybuf_ref[:, cols] = (y * ps_ref[:, cols]).astype(BF16)

    for rows, u in zip(tiles, us):
        v_ref[rows, :] = proj(u, o3, v_w).astype(BF16)
        sgb_ref[rows, :] = jax.nn.sigmoid(proj(u, o5, d_model)).astype(BF16)
    for rows, u in zip(tiles, us):
        ya = jnp.dot(ybuf_ref[rows, :], wa_ref[...], preferred_element_type=F32)
        ma_ref[rows, :] = (jax.nn.sigmoid(proj(u, o4, d_model)) * ya).astype(BF16)


def _proj_call(slopes, x2, mix_pre_g, w_in, pool_mix, pool_scale, w_branch_a, later_weights,
               *, seq, tm, sub, bias_period):
    t, d_model = x2.shape
    in_w = w_in.shape[1]
    pool_w = pool_scale.shape[1]
    v_w = pool_w
    qk_w = (in_w - pool_w - v_w - 2 * d_model) // 2
    steps = t // tm
    const = lambda i: (0, 0)
    row = lambda i: (i, 0)
    single = pl.Buffered(1)
    cast_steps, slab_in, slab_out, cast_shapes = [], [], [], []
    for w, layer in later_weights:
        _, rows_w, cols_w = w.shape
        n = max(d for d in range(1, steps + 1) if rows_w % (d * BF16_SUBLANES) == 0)
        block = (rows_w // n, cols_w)
        cast_steps.append(n)
        slab_in.append(pl.BlockSpec(
            (None,) + block, lambda i, n=n, layer=layer: (layer, jnp.minimum(i, n - 1), 0)))
        slab_out.append(pl.BlockSpec(block, lambda i, n=n: (jnp.minimum(i, n - 1), 0)))
        cast_shapes.append(jax.ShapeDtypeStruct((rows_w, cols_w), BF16))
    kern = functools.partial(_proj_kernel, cast_steps=tuple(cast_steps), tm=tm, sub=sub,
                             tiles_per_seq=seq // tm, pool_w=pool_w, qk_w=qk_w, v_w=v_w,
                             d_model=d_model, bias_period=bias_period)
    aug_w = qk_w // HEAD_WIDTH * (HEAD_WIDTH + AUG_LANES)
    outs = pl.pallas_call(
        kern,
        grid=(steps,),
        in_specs=[
            pl.BlockSpec(memory_space=pltpu.SMEM),
            pl.BlockSpec((tm, d_model), row),
            pl.BlockSpec((1, d_model), const),
            pl.BlockSpec((d_model, in_w), const, pipeline_mode=single),
            pl.BlockSpec(pool_mix.shape, lambda i: (0, 0, 0), pipeline_mode=single),
            pl.BlockSpec((1, pool_w), const),
            pl.BlockSpec((pool_w, d_model), const, pipeline_mode=single),
        ] + slab_in,
        out_specs=[
            pl.BlockSpec((tm, qk_w), row),
            pl.BlockSpec((tm, aug_w), row),
            pl.BlockSpec((tm, v_w), row),
            pl.BlockSpec((tm, d_model), row),
            pl.BlockSpec((tm, d_model), row),
        ] + slab_out,
        out_shape=[
            jax.ShapeDtypeStruct((t, qk_w), BF16),
            jax.ShapeDtypeStruct((t, aug_w), BF16),
            jax.ShapeDtypeStruct((t, v_w), BF16),
            jax.ShapeDtypeStruct((t, d_model), BF16),
            jax.ShapeDtypeStruct((t, d_model), BF16),
        ] + cast_shapes,
        scratch_shapes=[
            pltpu.VMEM((POOL_HALO + tm, pool_w), F32),
            pltpu.VMEM((tm, pool_w), BF16),
            pltpu.VMEM((bias_period // tm, tm, qk_w // HEAD_WIDTH * AUG_LANES), BF16),
        ],
        compiler_params=pltpu.CompilerParams(
            dimension_semantics=("arbitrary",), vmem_limit_bytes=VMEM_LIMIT_BYTES),
        name="proj_pool",
    )(slopes, x2, mix_pre_g, w_in, pool_mix, pool_scale, w_branch_a,
      *[w for w, _ in later_weights])
    return outs[:5], outs[5:]


def _attn_kernel(slopes_ref, lq1_ref, lk1_ref, lq2_ref, lk2_ref, sg_ref,
                 q_ref, kaug_ref, v_ref, o_ref, acc1_ref, acc2_ref, qx_ref, vaug_ref,
                 *, blk, rc, lambda_init):
    h = pl.program_id(1)
    qi = pl.program_id(2)
    slope = slopes_ref[h]
    rate = slope * LOG2E

    @pl.when(qi == 0)
    def _():
        vaug_ref[:, 0:HEAD_WIDTH] = v_ref[...]
        vaug_ref[:, HEAD_WIDTH:] = jnp.where(
            lax.broadcasted_iota(jnp.int32, (v_ref.shape[0], AUG_LANES), 1) == 0,
            1.0, 0.0).astype(BF16)
        qx_ref[...] = _aug_lanes(lax.broadcasted_iota(jnp.int32, (blk, AUG_LANES), 0),
                                 lax.broadcasted_iota(jnp.int32, (blk, AUG_LANES), 1),
                                 _rate_lanes(slope), True)

    lam = (jnp.exp(jnp.sum(lq1_ref[...] * lk1_ref[...], axis=-1, keepdims=True))
           - jnp.exp(jnp.sum(lq2_ref[...] * lk2_ref[...], axis=-1, keepdims=True))
           + lambda_init)

    q = q_ref[...]
    lane = lax.broadcasted_iota(jnp.int32, q.shape, 1)
    qx = qx_ref[...]
    zero = jnp.zeros_like(q)
    q1 = jnp.concatenate([jnp.where(lane < HEAD_DIM, q, zero), qx], axis=1)
    q2 = jnp.concatenate([jnp.where(lane >= HEAD_DIM, q, zero), qx], axis=1)

    acc1_ref[...] = jnp.zeros_like(acc1_ref)
    acc2_ref[...] = jnp.zeros_like(acc2_ref)

    maps = ((q1, acc1_ref), (q2, acc2_ref))

    lower_tri = (lax.broadcasted_iota(jnp.int32, (rc, rc), 0)
                 >= lax.broadcasted_iota(jnp.int32, (rc, rc), 1))

    def scores(qr, rows, kstart, nk, diagonal=False):
        kb = kaug_ref[pl.ds(kstart, nk), :]
        nt = (((1,), (1,)), ((), ()))
        s = lax.dot_general(qr[rows, :], kb, nt, preferred_element_type=F32)
        if diagonal:
            masked = jnp.where(lower_tri, s[:, nk - rc:], MASK_VALUE)
            s = masked if nk == rc else jnp.concatenate([s[:, :nk - rc], masked], axis=1)
        return s

    def update(s, ar, rows, kstart, nk, m, c):
        vb = vaug_ref[pl.ds(kstart, nk), :]
        m_new = jnp.maximum(m, jnp.max(s, axis=-1, keepdims=True) + c)
        alpha = jnp.exp2(m - m_new)
        p = jnp.exp2(s - (m_new - c))
        ar[rows, :] = alpha * ar[rows, :] + jnp.dot(p.astype(BF16), vb,
                                                    preferred_element_type=F32)
        return m_new

    every = slice(None)

    def bias_const(kj):
        return -rate * (qi * blk - (kj >> 1) * (2 * blk)).astype(F32)

    def block_pair(t, ms):
        start = pl.multiple_of(t * (2 * blk), 2 * blk)
        ss = [scores(qr, every, start, 2 * blk) for qr, _ in maps]
        return tuple(update(s, ar, every, start, 2 * blk, m, bias_const(2 * t))
                     for s, (_, ar), m in zip(ss, maps, ms))

    ms0 = tuple(jnp.full((blk, 1), MASK_VALUE, F32) for _ in maps)

    def tail(n_before, ms):
        start = pl.multiple_of(qi * blk - n_before, blk)
        chunks = [(slice(r * rc, (r + 1) * rc), n_before + (r + 1) * rc)
                  for r in range(blk // rc)]
        ss = [[scores(qr, rows, start, nk, diagonal=True) for qr, _ in maps]
              for rows, nk in chunks]
        for (rows, nk), s_pair in zip(chunks, ss):
            for s, (_, ar), m in zip(s_pair, maps, ms):
                update(s, ar, rows, start, nk, m[rows], bias_const(qi))
            a1 = acc1_ref[rows, :]
            a2 = acc2_ref[rows, :]
            o = (a1[:, :HEAD_WIDTH] / a1[:, HEAD_WIDTH:HEAD_WIDTH + 1]
                 - lam * (a2[:, :HEAD_WIDTH] / a2[:, HEAD_WIDTH:HEAD_WIDTH + 1]))
            o_ref[rows, :] = (_rms(o, sg_ref[...]) * (1.0 - lambda_init)).astype(BF16)

    n_pairs = qi // 2
    ms = lax.fori_loop(0, jnp.maximum(n_pairs - 1, 0), block_pair, ms0)

    def last_pair_and_tail(n_before):
        tail(n_before, block_pair(n_pairs - 1, ms))

    odd = qi % 2 == 1
    has_pair = n_pairs > 0
    pl.when(jnp.logical_and(has_pair, odd))(lambda: last_pair_and_tail(blk))
    pl.when(jnp.logical_and(has_pair, jnp.logical_not(odd)))(lambda: last_pair_and_tail(0))
    pl.when(jnp.logical_and(jnp.logical_not(has_pair), odd))(lambda: tail(blk, ms))
    pl.when(jnp.logical_and(jnp.logical_not(has_pair), jnp.logical_not(odd)))(
        lambda: tail(0, ms))


def _attn_call(q, k, v, slopes, lq1, lk1, lq2, lk2, subln_g, *, blk, rc, lambda_init):
    b, s, width = q.shape
    heads = width // HEAD_WIDTH
    small = lambda bi, hi, qi: (0, 0)
    kern = functools.partial(_attn_kernel, blk=blk, rc=rc, lambda_init=lambda_init)
    return pl.pallas_call(
        kern,
        grid=(b, heads, s // blk),
        in_specs=[
            pl.BlockSpec(memory_space=pltpu.SMEM),
            pl.BlockSpec((1, HEAD_DIM), small),
            pl.BlockSpec((1, HEAD_DIM), small),
            pl.BlockSpec((1, HEAD_DIM), small),
            pl.BlockSpec((1, HEAD_DIM), small),
            pl.BlockSpec((1, HEAD_WIDTH), small),
            pl.BlockSpec((None, blk, HEAD_WIDTH), lambda bi, hi, qi: (bi, qi, hi)),
            pl.BlockSpec((None, s, HEAD_WIDTH + AUG_LANES), lambda bi, hi, qi: (bi, 0, hi)),
            pl.BlockSpec((None, s, HEAD_WIDTH), lambda bi, hi, qi: (bi, 0, hi)),
        ],
        out_specs=pl.BlockSpec((None, blk, HEAD_WIDTH), lambda bi, hi, qi: (bi, qi, hi)),
        out_shape=jax.ShapeDtypeStruct((b, s, width), BF16),
        scratch_shapes=[
            pltpu.VMEM((blk, HEAD_WIDTH + AUG_LANES), F32),
            pltpu.VMEM((blk, HEAD_WIDTH + AUG_LANES), F32),
            pltpu.VMEM((blk, AUG_LANES), BF16),
            pltpu.VMEM((s, HEAD_WIDTH + AUG_LANES), BF16),
        ],
        compiler_params=pltpu.CompilerParams(
            dimension_semantics=("parallel", "parallel", "arbitrary"),
            vmem_limit_bytes=VMEM_LIMIT_BYTES),
        name="diff_attn",
    )(slopes, lq1, lk1, lq2, lk2, subln_g, q, k, v)


def _out_ffn_kernel(x_ref, o_ref, ma_ref, sgb_ref, wb_ref, wo_ref, g_post_ref,
                    g_pre_ref, g_fpost_ref, wg_ref, wu_ref, wd_ref, out_ref, *, sub):
    tiles = [slice(r0, r0 + sub) for r0 in range(0, x_ref.shape[0], sub)]

    def branch_b(rows):
        return jnp.dot(o_ref[rows, :], wb_ref[...], preferred_element_type=F32)

    def mix(rows, yb):
        m = ma_ref[rows, :].astype(F32) + sgb_ref[rows, :].astype(F32) * yb
        return jnp.dot(m.astype(BF16), wo_ref[...], preferred_element_type=F32)

    def norms(rows, mo):
        h1 = x_ref[rows, :] + _rms(mo, g_post_ref[...])
        return h1, _rms(h1, g_pre_ref[...]).astype(BF16)

    def hidden(u):
        gate = jnp.dot(u, wg_ref[...], preferred_element_type=F32)
        up = jnp.dot(u, wu_ref[...], preferred_element_type=F32)
        return (jax.nn.silu(gate) * up).astype(BF16)

    ybs = [branch_b(rows) for rows in tiles]
    mos = [mix(rows, yb) for rows, yb in zip(tiles, ybs)]
    hus = [norms(rows, mo) for rows, mo in zip(tiles, mos)]
    fs = [hidden(u) for _, u in hus]
    dns = [jnp.dot(f, wd_ref[...], preferred_element_type=F32) for f in fs]
    for rows, (h1, _), dn in zip(tiles, hus, dns):
        out_ref[rows, :] = h1 + _rms(dn, g_fpost_ref[...])


def _out_ffn_call(x2, o2, ma, sgb, w_branch_b, w_out, mix_post_g, ffn_pre_g, ffn_post_g,
                  w_gate, w_up, w_down, *, tm, sub):
    t, d_model = x2.shape
    const = lambda i: (0, 0)
    row = lambda i: (i, 0)
    single = pl.Buffered(1)

    def wspec(w):
        return pl.BlockSpec(w.shape, const, pipeline_mode=single)

    return pl.pallas_call(
        functools.partial(_out_ffn_kernel, sub=sub),
        grid=(t // tm,),
        in_specs=[
            pl.BlockSpec((tm, d_model), row),
            pl.BlockSpec((tm, o2.shape[1]), row),
            pl.BlockSpec((tm, d_model), row),
            pl.BlockSpec((tm, d_model), row),
            wspec(w_branch_b),
            wspec(w_out),
            pl.BlockSpec((1, d_model), const),
            pl.BlockSpec((1, d_model), const),
            pl.BlockSpec((1, d_model), const),
            wspec(w_gate),
            wspec(w_up),
            wspec(w_down),
        ],
        out_specs=pl.BlockSpec((tm, d_model), row),
        out_shape=jax.ShapeDtypeStruct((t, d_model), F32),
        compiler_params=pltpu.CompilerParams(
            dimension_semantics=("parallel",), vmem_limit_bytes=VMEM_LIMIT_BYTES),
        name="out_ffn",
    )(x2, o2, ma, sgb, w_branch_b, w_out, mix_post_g, ffn_pre_g, ffn_post_g,
      w_gate, w_up, w_down)


def _alibi_slopes(n):
    start = 2.0 ** (-8.0 / n)
    return jnp.asarray([start ** (i + 1) for i in range(n)], dtype=F32)


def _layer(h, layer_idx, w_in, pool_mix, pool_scale, w_branch_a, lam_q1, lam_k1, lam_q2,
           lam_k2, subln_g, w_branch_b, w_out, mix_pre_g, mix_post_g, ffn_pre_g, ffn_post_g,
           w_ffn_gate, w_ffn_up, w_ffn_down):
    b, s, d_model = h.shape
    lambda_init = 0.8 - 0.6 * math.exp(-0.3 * layer_idx)
    x2 = h.reshape(b * s, d_model)
    tiles = _tiling(s)
    row = lambda a: a.reshape(1, -1)

    heads = w_branch_b.shape[1] // HEAD_WIDTH
    slopes = _alibi_slopes(heads)
    (q, k, v, ma, sgb), (wb16, wo16, wg16, wu16, wd16) = _proj_call(
        slopes, x2, row(mix_pre_g), w_in.astype(BF16), pool_mix.astype(BF16),
        row(pool_scale), w_branch_a.astype(BF16),
        [(w, layer_idx) for w in (w_branch_b, w_out, w_ffn_gate, w_ffn_up, w_ffn_down)],
        seq=s, tm=tiles["proj_tm"], sub=tiles["sub"], bias_period=2 * tiles["attn_blk"])

    width = q.shape[1]
    o = _attn_call(q.reshape(b, s, width), k.reshape(b, s, -1), v.reshape(b, s, -1),
                   slopes, row(lam_q1), row(lam_k1), row(lam_q2), row(lam_k2),
                   row(subln_g), blk=tiles["attn_blk"], rc=tiles["attn_rc"],
                   lambda_init=lambda_init)

    out = _out_ffn_call(
        x2, o.reshape(b * s, width), ma, sgb, wb16, wo16,
        row(mix_post_g), row(ffn_pre_g), row(ffn_post_g), wg16, wu16, wd16,
        tm=tiles["ffn_tm"], sub=tiles["sub"])
    return out.reshape(b, s, d_model)


def kernel(x, w_in, pool_mix, pool_scale, w_branch_a, lam_q1, lam_k1, lam_q2, lam_k2,
           subln_g, w_branch_b, w_out, mix_pre_g, mix_post_g, ffn_pre_g, ffn_post_g,
           w_ffn_gate, w_ffn_up, w_ffn_down):
    h = x
    for l in range(w_in.shape[0]):
        h = _layer(h, l, w_in[l], pool_mix[l], pool_scale[l], w_branch_a[l], lam_q1[l],
                   lam_k1[l], lam_q2[l], lam_k2[l], subln_g[l], w_branch_b, w_out,
                   mix_pre_g[l], mix_post_g[l], ffn_pre_g[l], ffn_post_g[l], w_ffn_gate,
                   w_ffn_up, w_ffn_down)
    return h
```

```python
import functools
import math

import jax
import jax.numpy as jnp
from jax import lax
from jax.experimental import pallas as pl
from jax.experimental.pallas import tpu as pltpu

NORM_EPS = 1e-6
POOL_WINDOWS = (2, 4, 8, 16)
POOL_GROUP_DIM = 128
POOL_HALO = 16
HEAD_DIM = 64
HEAD_WIDTH = 2 * HEAD_DIM
MASK_VALUE = -1e30
AUG_LANES = 128
BF16_EXACT_INT = 256
BF16_SUBLANES = 16
N_BIAS_LANES = 6
LOG2E = math.log2(math.e)
BF16 = jnp.bfloat16
F32 = jnp.float32

VMEM_LIMIT_BYTES = 52 * 1024 * 1024
MXU_ROWS = 256


def _tiling(seq):
    proj_tm = 4 * MXU_ROWS
    attn_blk = 4 * MXU_ROWS
    ffn_tm = 2 * MXU_ROWS
    assert seq % proj_tm == 0 and seq % (2 * attn_blk) == 0 and seq % ffn_tm == 0
    assert attn_blk >= BF16_EXACT_INT and AUG_LANES == HEAD_WIDTH
    return dict(proj_tm=proj_tm, attn_blk=attn_blk, attn_rc=MXU_ROWS, ffn_tm=ffn_tm,
                sub=MXU_ROWS)


def _rms(x, g):
    ms = jnp.mean(x * x, axis=-1, keepdims=True)
    return x * lax.rsqrt(ms + NORM_EPS) * g


def _rate_lanes(slope):
    lane_row = lax.broadcasted_iota(jnp.int32, (1, AUG_LANES), 1)
    part = jnp.where(lane_row >= N_BIAS_LANES, lane_row - N_BIAS_LANES, lane_row) >> 1
    r0 = jnp.full((1, AUG_LANES), slope * LOG2E, F32)
    r_hi = r0.astype(BF16).astype(F32)
    r_mid = (r0 - r_hi).astype(BF16).astype(F32)
    r_lo = (r0 - r_hi - r_mid).astype(BF16).astype(F32)
    return jnp.where(part == 0, r_hi, jnp.where(part == 1, r_mid, r_lo))


def _aug_lanes(idx, lane, rate_lanes, negate_pos):
    lo = idx & (BF16_EXACT_INT - 1)
    pos = jnp.where((lane & 1) == 0, idx - lo, lo).astype(F32)
    first = lane < N_BIAS_LANES
    second = jnp.logical_and(lane >= N_BIAS_LANES, lane < 2 * N_BIAS_LANES)
    if negate_pos:
        x = jnp.where(first, -pos, jnp.where(second, rate_lanes, 0.0))
    else:
        x = jnp.where(first, rate_lanes, jnp.where(second, pos, 0.0))
    return x.astype(BF16)


def _proj_kernel(slopes_ref, x_ref, g_ref, w_ref, pm_ref, ps_ref, wa_ref, *rest,
                 cast_steps, tm, sub, tiles_per_seq, pool_w, qk_w, v_w, d_model, bias_period):
    n_cast = len(cast_steps)
    cast_in = rest[:n_cast]
    q_ref, k_ref, v_ref, ma_ref, sgb_ref = rest[n_cast:n_cast + 5]
    cast_out = rest[n_cast + 5:2 * n_cast + 5]
    pbuf_ref, ybuf_ref, kx_ref = rest[2 * n_cast + 5:]
    i = pl.program_id(0)
    seq_tile = i % tiles_per_seq
    for n in sorted(set(cast_steps)):
        @pl.when(i < n)
        def _(n=n):
            for src, dst, n_w in zip(cast_in, cast_out, cast_steps):
                if n_w == n:
                    dst[...] = src[...].astype(BF16)

    tiles = [slice(r0, r0 + sub) for r0 in range(0, tm, sub)]

    def proj(u, lo, width):
        return jnp.dot(u, w_ref[:, lo:lo + width], preferred_element_type=F32)

    n_heads = qk_w // HEAD_WIDTH
    tiles_per_period = bias_period // tm

    @pl.when(i == 0)
    def _():
        lane = lax.broadcasted_iota(jnp.int32, (tm, AUG_LANES), 1)
        row = lax.broadcasted_iota(jnp.int32, (tm, AUG_LANES), 0)
        for h in range(n_heads):
            rate = _rate_lanes(slopes_ref[h])
            for p in range(tiles_per_period):
                kx_ref[p, :, h * AUG_LANES:(h + 1) * AUG_LANES] = _aug_lanes(
                    row + p * tm, lane, rate, False)

    phase = seq_tile % tiles_per_period
    ones_col = jnp.where(lax.broadcasted_iota(jnp.int32, (sub, AUG_LANES), 1) == 0,
                         1.0, 0.0).astype(BF16)

    def key_extra(rows, h):
        return kx_ref[phase, rows, h * AUG_LANES:(h + 1) * AUG_LANES]

    def value_extra(rows, h):
        return ones_col

    def store_heads(dst_ref, rows, data, extra):
        for h in range(n_heads):
            c0 = h * (HEAD_WIDTH + AUG_LANES)
            dst_ref[rows, c0:c0 + HEAD_WIDTH] = data[:, h * HEAD_WIDTH:(h + 1) * HEAD_WIDTH]
            dst_ref[rows, c0 + HEAD_WIDTH:c0 + HEAD_WIDTH + AUG_LANES] = extra(rows, h)

    o1 = pool_w
    o2 = o1 + qk_w
    o3 = o2 + qk_w
    o4 = o3 + v_w
    o5 = o4 + d_model
    @pl.when(seq_tile == 0)
    def _():
        pbuf_ref[0:POOL_HALO, :] = jnp.zeros((POOL_HALO, pool_w), F32)

    @pl.when(seq_tile != 0)
    def _():
        pbuf_ref[0:POOL_HALO, :] = pbuf_ref[tm:tm + POOL_HALO, :]

    us = []
    for rows in tiles:
        us.append(_rms(x_ref[rows, :], g_ref[...]).astype(BF16))
        pbuf_ref[POOL_HALO + rows.start:POOL_HALO + rows.stop, :] = proj(us[-1], 0, pool_w)
    for rows, u in zip(tiles, us):
        store_heads(v_ref, rows, proj(u, o3, v_w).astype(BF16), value_extra)
        sgb_ref[rows, :] = jax.nn.sigmoid(proj(u, o5, d_model)).astype(BF16)

    t_loc = seq_tile * tm + lax.broadcasted_iota(jnp.int32, (tm, 1), 0)
    for g, w in enumerate(POOL_WINDOWS):
        cols = slice(g * POOL_GROUP_DIM, (g + 1) * POOL_GROUP_DIM)
        run = pbuf_ref[:, cols]
        d = 1
        while d < w:
            run = run + pltpu.roll(run, d, 0)
            d *= 2
        tok = pbuf_ref[POOL_HALO:POOL_HALO + tm, cols]
        cnt = jnp.minimum(t_loc + 1, w).astype(F32)
        pooled = run[POOL_HALO:, :] / cnt - tok
        y = jnp.dot(pooled.astype(BF16), pm_ref[g], preferred_element_type=F32)
        ybuf_ref[:, cols] = (y * ps_ref[:, cols]).astype(BF16)

    for rows, u in zip(tiles, us):
        ya = jnp.dot(ybuf_ref[rows, :], wa_ref[...], preferred_element_type=F32)
        ma_ref[rows, :] = (jax.nn.sigmoid(proj(u, o4, d_model)) * ya).astype(BF16)
    for rows, u in zip(tiles, us):
        q_ref[rows, :] = (proj(u, o1, qk_w) * (LOG2E / math.sqrt(HEAD_DIM))).astype(BF16)
        store_heads(k_ref, rows, proj(u, o2, qk_w).astype(BF16), key_extra)


def _proj_call(slopes, x2, mix_pre_g, w_in, pool_mix, pool_scale, w_branch_a, later_weights,
               *, seq, tm, sub, bias_period):
    t, d_model = x2.shape
    in_w = w_in.shape[1]
    pool_w = pool_scale.shape[1]
    v_w = pool_w
    qk_w = (in_w - pool_w - v_w - 2 * d_model) // 2
    steps = t // tm
    const = lambda i: (0, 0)
    row = lambda i: (i, 0)
    single = pl.Buffered(1)
    cast_steps, slab_in, slab_out, cast_shapes = [], [], [], []
    for w, layer in later_weights:
        _, rows_w, cols_w = w.shape
        n = max(d for d in range(1, steps + 1) if rows_w % (d * BF16_SUBLANES) == 0)
        block = (rows_w // n, cols_w)
        cast_steps.append(n)
        slab_in.append(pl.BlockSpec(
            (None,) + block, lambda i, n=n, layer=layer: (layer, jnp.minimum(i, n - 1), 0)))
        slab_out.append(pl.BlockSpec(block, lambda i, n=n: (jnp.minimum(i, n - 1), 0)))
        cast_shapes.append(jax.ShapeDtypeStruct((rows_w, cols_w), BF16))
    kern = functools.partial(_proj_kernel, cast_steps=tuple(cast_steps), tm=tm, sub=sub,
                             tiles_per_seq=seq // tm, pool_w=pool_w, qk_w=qk_w, v_w=v_w,
                             d_model=d_model, bias_period=bias_period)
    aug_w = qk_w // HEAD_WIDTH * (HEAD_WIDTH + AUG_LANES)
    outs = pl.pallas_call(
        kern,
        grid=(steps,),
        in_specs=[
            pl.BlockSpec(memory_space=pltpu.SMEM),
            pl.BlockSpec((tm, d_model), row),
            pl.BlockSpec((1, d_model), const),
            pl.BlockSpec((d_model, in_w), const, pipeline_mode=single),
            pl.BlockSpec(pool_mix.shape, lambda i: (0, 0, 0), pipeline_mode=single),
            pl.BlockSpec((1, pool_w), const),
            pl.BlockSpec((pool_w, d_model), const, pipeline_mode=single),
        ] + slab_in,
        out_specs=[
            pl.BlockSpec((tm, qk_w), row),
            pl.BlockSpec((tm, aug_w), row),
            pl.BlockSpec((tm, aug_w), row),
            pl.BlockSpec((tm, d_model), row),
            pl.BlockSpec((tm, d_model), row),
        ] + slab_out,
        out_shape=[
            jax.ShapeDtypeStruct((t, qk_w), BF16),
            jax.ShapeDtypeStruct((t, aug_w), BF16),
            jax.ShapeDtypeStruct((t, aug_w), BF16),
            jax.ShapeDtypeStruct((t, d_model), BF16),
            jax.ShapeDtypeStruct((t, d_model), BF16),
        ] + cast_shapes,
        scratch_shapes=[
            pltpu.VMEM((POOL_HALO + tm, pool_w), F32),
            pltpu.VMEM((tm, pool_w), BF16),
            pltpu.VMEM((bias_period // tm, tm, qk_w // HEAD_WIDTH * AUG_LANES), BF16),
        ],
        compiler_params=pltpu.CompilerParams(
            dimension_semantics=("arbitrary",), vmem_limit_bytes=VMEM_LIMIT_BYTES),
        name="proj_pool",
    )(slopes, x2, mix_pre_g, w_in, pool_mix, pool_scale, w_branch_a,
      *[w for w, _ in later_weights])
    return outs[:5], outs[5:]


def _attn_kernel(slopes_ref, lq1_ref, lk1_ref, lq2_ref, lk2_ref, sg_ref,
                 q_ref, kaug_ref, vaug_ref, o_ref, acc1_ref, acc2_ref, qx_ref,
                 *, blk, rc, lambda_init):
    h = pl.program_id(1)
    qi = pl.program_id(2)
    slope = slopes_ref[h]
    rate = slope * LOG2E

    @pl.when(qi == 0)
    def _():
        qx_ref[...] = _aug_lanes(lax.broadcasted_iota(jnp.int32, (blk, AUG_LANES), 0),
                                 lax.broadcasted_iota(jnp.int32, (blk, AUG_LANES), 1),
                                 _rate_lanes(slope), True)

    lam = (jnp.exp(jnp.sum(lq1_ref[...] * lk1_ref[...], axis=-1, keepdims=True))
           - jnp.exp(jnp.sum(lq2_ref[...] * lk2_ref[...], axis=-1, keepdims=True))
           + lambda_init)

    q = q_ref[...]
    lane = lax.broadcasted_iota(jnp.int32, q.shape, 1)
    qx = qx_ref[...]
    zero = jnp.zeros_like(q)
    q1 = jnp.concatenate([jnp.where(lane < HEAD_DIM, q, zero), qx], axis=1)
    q2 = jnp.concatenate([jnp.where(lane >= HEAD_DIM, q, zero), qx], axis=1)

    acc1_ref[...] = jnp.zeros_like(acc1_ref)
    acc2_ref[...] = jnp.zeros_like(acc2_ref)

    maps = ((q1, acc1_ref), (q2, acc2_ref))

    lower_tri = (lax.broadcasted_iota(jnp.int32, (rc, rc), 0)
                 >= lax.broadcasted_iota(jnp.int32, (rc, rc), 1))

    def scores(qr, rows, kstart, nk, diagonal=False):
        kb = kaug_ref[pl.ds(kstart, nk), :]
        nt = (((1,), (1,)), ((), ()))
        s = lax.dot_general(qr[rows, :], kb, nt, preferred_element_type=F32)
        if diagonal:
            masked = jnp.where(lower_tri, s[:, nk - rc:], MASK_VALUE)
            s = masked if nk == rc else jnp.concatenate([s[:, :nk - rc], masked], axis=1)
        return s

    def update(s, ar, rows, kstart, nk, m, c):
        vb = vaug_ref[pl.ds(kstart, nk), :]
        m_new = jnp.maximum(m, jnp.max(s, axis=-1, keepdims=True) + c)
        alpha = jnp.exp2(m - m_new)
        p = jnp.exp2(s - (m_new - c))
        ar[rows, :] = alpha * ar[rows, :] + jnp.dot(p.astype(BF16), vb,
                                                    preferred_element_type=F32)
        return m_new

    every = slice(None)

    def bias_const(kj):
        return -rate * (qi * blk - (kj >> 1) * (2 * blk)).astype(F32)

    def block_pair(t, ms):
        start = pl.multiple_of(t * (2 * blk), 2 * blk)
        ss = [scores(qr, every, start, 2 * blk) for qr, _ in maps]
        return tuple(update(s, ar, every, start, 2 * blk, m, bias_const(2 * t))
                     for s, (_, ar), m in zip(ss, maps, ms))

    ms0 = tuple(jnp.full((blk, 1), MASK_VALUE, F32) for _ in maps)

    def tail(n_before, ms):
        start = pl.multiple_of(qi * blk - n_before, blk)
        chunks = [(slice(r * rc, (r + 1) * rc), n_before + (r + 1) * rc)
                  for r in range(blk // rc)]
        ss = [[scores(qr, rows, start, nk, diagonal=True) for qr, _ in maps]
              for rows, nk in chunks]
        for (rows, nk), s_pair in zip(chunks, ss):
            for s, (_, ar), m in zip(s_pair, maps, ms):
                update(s, ar, rows, start, nk, m[rows], bias_const(qi))
            a1 = acc1_ref[rows, :]
            a2 = acc2_ref[rows, :]
            o = (a1[:, :HEAD_WIDTH] / a1[:, HEAD_WIDTH:HEAD_WIDTH + 1]
                 - lam * (a2[:, :HEAD_WIDTH] / a2[:, HEAD_WIDTH:HEAD_WIDTH + 1]))
            o_ref[rows, :] = (_rms(o, sg_ref[...]) * (1.0 - lambda_init)).astype(BF16)

    n_pairs = qi // 2
    ms = lax.fori_loop(0, jnp.maximum(n_pairs - 1, 0), block_pair, ms0)

    def last_pair_and_tail(n_before):
        tail(n_before, block_pair(n_pairs - 1, ms))

    odd = qi % 2 == 1
    has_pair = n_pairs > 0
    pl.when(jnp.logical_and(has_pair, odd))(lambda: last_pair_and_tail(blk))
    pl.when(jnp.logical_and(has_pair, jnp.logical_not(odd)))(lambda: last_pair_and_tail(0))
    pl.when(jnp.logical_and(jnp.logical_not(has_pair), odd))(lambda: tail(blk, ms))
    pl.when(jnp.logical_and(jnp.logical_not(has_pair), jnp.logical_not(odd)))(
        lambda: tail(0, ms))


def _attn_call(q, k, v, slopes, lq1, lk1, lq2, lk2, subln_g, *, blk, rc, lambda_init):
    b, s, width = q.shape
    heads = width // HEAD_WIDTH
    small = lambda bi, hi, qi: (0, 0)
    kern = functools.partial(_attn_kernel, blk=blk, rc=rc, lambda_init=lambda_init)
    return pl.pallas_call(
        kern,
        grid=(b, heads, s // blk),
        in_specs=[
            pl.BlockSpec(memory_space=pltpu.SMEM),
            pl.BlockSpec((1, HEAD_DIM), small),
            pl.BlockSpec((1, HEAD_DIM), small),
            pl.BlockSpec((1, HEAD_DIM), small),
            pl.BlockSpec((1, HEAD_DIM), small),
            pl.BlockSpec((1, HEAD_WIDTH), small),
            pl.BlockSpec((None, blk, HEAD_WIDTH), lambda bi, hi, qi: (bi, qi, hi)),
            pl.BlockSpec((None, s, HEAD_WIDTH + AUG_LANES), lambda bi, hi, qi: (bi, 0, hi)),
            pl.BlockSpec((None, s, HEAD_WIDTH + AUG_LANES), lambda bi, hi, qi: (bi, 0, hi)),
        ],
        out_specs=pl.BlockSpec((None, blk, HEAD_WIDTH), lambda bi, hi, qi: (bi, qi, hi)),
        out_shape=jax.ShapeDtypeStruct((b, s, width), BF16),
        scratch_shapes=[
            pltpu.VMEM((blk, HEAD_WIDTH + AUG_LANES), F32),
            pltpu.VMEM((blk, HEAD_WIDTH + AUG_LANES), F32),
            pltpu.VMEM((blk, AUG_LANES), BF16),
        ],
        compiler_params=pltpu.CompilerParams(
            dimension_semantics=("parallel", "parallel", "arbitrary"),
            vmem_limit_bytes=VMEM_LIMIT_BYTES),
        name="diff_attn",
    )(slopes, lq1, lk1, lq2, lk2, subln_g, q, k, v)


def _out_ffn_kernel(x_ref, o_ref, ma_ref, sgb_ref, wb_ref, wo_ref, g_post_ref,
                    g_pre_ref, g_fpost_ref, wg_ref, wu_ref, wd_ref, out_ref, *, sub):
    tiles = [slice(r0, r0 + sub) for r0 in range(0, x_ref.shape[0], sub)]

    def branch_b(rows):
        return jnp.dot(o_ref[rows, :], wb_ref[...], preferred_element_type=F32)

    def mix(rows, yb):
        m = ma_ref[rows, :].astype(F32) + sgb_ref[rows, :].astype(F32) * yb
        return jnp.dot(m.astype(BF16), wo_ref[...], preferred_element_type=F32)

    def norms(rows, mo):
        h1 = x_ref[rows, :] + _rms(mo, g_post_ref[...])
        return h1, _rms(h1, g_pre_ref[...]).astype(BF16)

    def hidden(u):
        gate = jnp.dot(u, wg_ref[...], preferred_element_type=F32)
        up = jnp.dot(u, wu_ref[...], preferred_element_type=F32)
        return (jax.nn.silu(gate) * up).astype(BF16)

    ybs = [branch_b(rows) for rows in tiles]
    mos = [mix(rows, yb) for rows, yb in zip(tiles, ybs)]
    hus = [norms(rows, mo) for rows, mo in zip(tiles, mos)]
    fs = [hidden(u) for _, u in hus]
    dns = [jnp.dot(f, wd_ref[...], preferred_element_type=F32) for f in fs]
    for rows, (h1, _), dn in zip(tiles, hus, dns):
        out_ref[rows, :] = h1 + _rms(dn, g_fpost_ref[...])


def _out_ffn_call(x2, o2, ma, sgb, w_branch_b, w_out, mix_post_g, ffn_pre_g, ffn_post_g,
                  w_gate, w_up, w_down, *, tm, sub):
    t, d_model = x2.shape
    const = lambda i: (0, 0)
    row = lambda i: (i, 0)
    single = pl.Buffered(1)

    def wspec(w):
        return pl.BlockSpec(w.shape, const, pipeline_mode=single)

    return pl.pallas_call(
        functools.partial(_out_ffn_kernel, sub=sub),
        grid=(t // tm,),
        in_specs=[
            pl.BlockSpec((tm, d_model), row),
            pl.BlockSpec((tm, o2.shape[1]), row),
            pl.BlockSpec((tm, d_model), row),
            pl.BlockSpec((tm, d_model), row),
            wspec(w_branch_b),
            wspec(w_out),
            pl.BlockSpec((1, d_model), const),
            pl.BlockSpec((1, d_model), const),
            pl.BlockSpec((1, d_model), const),
            wspec(w_gate),
            wspec(w_up),
            wspec(w_down),
        ],
        out_specs=pl.BlockSpec((tm, d_model), row),
        out_shape=jax.ShapeDtypeStruct((t, d_model), F32),
        compiler_params=pltpu.CompilerParams(
            dimension_semantics=("parallel",), vmem_limit_bytes=VMEM_LIMIT_BYTES),
        name="out_ffn",
    )(x2, o2, ma, sgb, w_branch_b, w_out, mix_post_g, ffn_pre_g, ffn_post_g,
      w_gate, w_up, w_down)


def _alibi_slopes(n):
    start = 2.0 ** (-8.0 / n)
    return jnp.asarray([start ** (i + 1) for i in range(n)], dtype=F32)


def _layer(h, layer_idx, w_in, pool_mix, pool_scale, w_branch_a, lam_q1, lam_k1, lam_q2,
           lam_k2, subln_g, w_branch_b, w_out, mix_pre_g, mix_post_g, ffn_pre_g, ffn_post_g,
           w_ffn_gate, w_ffn_up, w_ffn_down):
    b, s, d_model = h.shape
    lambda_init = 0.8 - 0.6 * math.exp(-0.3 * layer_idx)
    x2 = h.reshape(b * s, d_model)
    tiles = _tiling(s)
    row = lambda a: a.reshape(1, -1)

    heads = w_branch_b.shape[1] // HEAD_WIDTH
    slopes = _alibi_slopes(heads)
    (q, k, v, ma, sgb), (wb16, wo16, wg16, wu16, wd16) = _proj_call(
        slopes, x2, row(mix_pre_g), w_in.astype(BF16), pool_mix.astype(BF16),
        row(pool_scale), w_branch_a.astype(BF16),
        [(w, layer_idx) for w in (w_branch_b, w_out, w_ffn_gate, w_ffn_up, w_ffn_down)],
        seq=s, tm=tiles["proj_tm"], sub=tiles["sub"], bias_period=2 * tiles["attn_blk"])

    width = q.shape[1]
    o = _attn_call(q.reshape(b, s, width), k.reshape(b, s, -1), v.reshape(b, s, -1),
                   slopes, row(lam_q1), row(lam_k1), row(lam_q2), row(lam_k2),
                   row(subln_g), blk=tiles["attn_blk"], rc=tiles["attn_rc"],
                   lambda_init=lambda_init)

    out = _out_ffn_call(
        x2, o.reshape(b * s, width), ma, sgb, wb16, wo16,
        row(mix_post_g), row(ffn_pre_g), row(ffn_post_g), wg16, wu16, wd16,
        tm=tiles["ffn_tm"], sub=tiles["sub"])
    return out.reshape(b, s, d_model)


def kernel(x, w_in, pool_mix, pool_scale, w_branch_a, lam_q1, lam_k1, lam_q2, lam_k2,
           subln_g, w_branch_b, w_out, mix_pre_g, mix_post_g, ffn_pre_g, ffn_post_g,
           w_ffn_gate, w_ffn_up, w_ffn_down):
    h = x
    for l in range(w_in.shape[0]):
        h = _layer(h, l, w_in[l], pool_mix[l], pool_scale[l], w_branch_a[l], lam_q1[l],
                   lam_k1[l], lam_q2[l], lam_k2[l], subln_g[l], w_branch_b, w_out,
                   mix_pre_g[l], mix_post_g[l], ffn_pre_g[l], ffn_post_g[l], w_ffn_gate,
                   w_ffn_up, w_ffn_down)
    return h
```
